```python
import math
import jax
import jax.numpy as jnp
from jax import lax
import numpy as np

D_MODEL = 2048
BATCH = 16
SEQ = 2048
DEPTH = 4
DEC_BATCH = 8
DEC_SEQ = 64
PAST_LEN = 2048

CHUNK = 64
N_MIXERS = 4
N_A = (DEPTH + 3) // N_MIXERS
N_B = (DEPTH + 2) // N_MIXERS
N_C = (DEPTH + 1) // N_MIXERS
N_D = DEPTH // N_MIXERS
EPS = 1e-6

CONV_A_WIDTH = 31
DN_HEAD_DIM = 128
DN_QK_HEADS = D_MODEL // DN_HEAD_DIM
DN_V_HEADS = 2 * DN_QK_HEADS
DN_QK_DIM = DN_QK_HEADS * DN_HEAD_DIM
DN_V_DIM = DN_V_HEADS * DN_HEAD_DIM
DN_QKV_DIM = 2 * DN_QK_DIM + DN_V_DIM
DN_IN_DIM = DN_QKV_DIM + DN_V_DIM + 2 * DN_V_HEADS
DN_CONV_WIDTH = 4
SC_WIDTH = 3
SWA_HEAD_DIM = 64
SWA_HEADS = D_MODEL // SWA_HEAD_DIM
SWA_KV_HEADS = 8
SWA_GROUP = SWA_HEADS // SWA_KV_HEADS
SWA_Q_DIM = SWA_HEADS * SWA_HEAD_DIM
SWA_KV_DIM = SWA_KV_HEADS * SWA_HEAD_DIM
WINDOW = 128
REL_BUCKETS = 32
REL_MAX_DIST = 128
FFN_HIDDEN = ((8 * D_MODEL // 3 + 255) // 256) * 256

kernel_name = "hybrid_chunk_causal_encoder_step"


def rmsnorm(x, g):
    xf = x.astype(jnp.float32)
    y = xf * lax.rsqrt(jnp.mean(xf * xf, axis=-1, keepdims=True) + EPS)
    return (y * g.astype(jnp.float32)).astype(x.dtype)


def layernorm(x, g, b):
    xf = x.astype(jnp.float32)
    xc = xf - jnp.mean(xf, axis=-1, keepdims=True)
    y = xc * lax.rsqrt(jnp.mean(xc * xc, axis=-1, keepdims=True) + EPS)
    return (y * g.astype(jnp.float32) + b.astype(jnp.float32)).astype(x.dtype)


def l2norm(x):
    return x * lax.rsqrt(jnp.sum(x * x, axis=-1, keepdims=True) + EPS)


def causal_dwconv(ext, w):
    return lax.conv_general_dilated(
        ext, w[:, None, :].astype(ext.dtype), window_strides=(1,), padding="VALID",
        dimension_numbers=("NWC", "WIO", "NWC"), feature_group_count=ext.shape[-1])


def swiglu(h, wg, wu, wd):
    return (jax.nn.silu(h @ wg) * (h @ wu)) @ wd


def conformer_conv(h, buf, w1, b1, dw, dwb, ln_g, ln_b, w2, b2):
    u = h @ w1 + b1
    u = u[..., :D_MODEL] * jax.nn.sigmoid(u[..., D_MODEL:])
    ext = jnp.concatenate([buf.astype(u.dtype), u], axis=1)
    c = layernorm(causal_dwconv(ext, dw) + dwb, ln_g, ln_b)
    return jax.nn.silu(c) @ w2 + b2, ext[:, -(CONV_A_WIDTH - 1):]


def gated_delta_rule(q, k, v, g, beta, s0, chunk):
    bsz, t, nh, _ = q.shape
    dv = v.shape[-1]
    n = t // chunk

    def blocks(a):
        a = a.reshape((bsz, n, chunk, nh) + a.shape[3:])
        return jnp.swapaxes(jnp.moveaxis(a, 1, 0), 2, 3)

    qb, kb, vb, gb, bb = blocks(q), blocks(k), blocks(v), blocks(g), blocks(beta)
    gc = jnp.cumsum(gb, axis=-1)
    pos = jnp.arange(chunk)
    incl = pos[:, None] >= pos[None, :]
    strict = pos[:, None] > pos[None, :]
    decay = jnp.exp(jnp.where(incl, gc[..., :, None] - gc[..., None, :], -jnp.inf))
    m = jnp.where(strict, bb[..., :, None] * jnp.einsum("nbhid,nbhjd->nbhij", kb, kb) * decay, 0.0)
    a = m + jnp.eye(chunk, dtype=m.dtype)
    rhs = jnp.concatenate([bb[..., None] * vb, (bb * jnp.exp(gc))[..., None] * kb], axis=-1)
    sol = lax.linalg.triangular_solve(a, rhs, left_side=True, lower=True, unit_diagonal=True)
    u, w = sol[..., :dv], sol[..., dv:]
    qk = jnp.where(incl, jnp.einsum("nbhid,nbhjd->nbhij", qb, kb) * decay, 0.0)
    qg = qb * jnp.exp(gc)[..., None]
    kg = kb * jnp.exp(gc[..., -1:] - gc)[..., None]
    glast = jnp.exp(gc[..., -1])

    def step(s, xs):
        u_c, w_c, qk_c, qg_c, kg_c, gl_c = xs
        v_new = u_c - jnp.einsum("bhik,bhkv->bhiv", w_c, s)
        o = jnp.einsum("bhik,bhkv->bhiv", qg_c, s) + jnp.einsum("bhij,bhjv->bhiv", qk_c, v_new)
        s = s * gl_c[..., None, None] + jnp.einsum("bhik,bhiv->bhkv", kg_c, v_new)
        return s, o

    s_fin, o = lax.scan(step, s0, (u, w, qk, qg, kg, glast))
    o = jnp.moveaxis(jnp.swapaxes(o, 2, 3), 0, 1).reshape(bsz, t, nh, dv)
    return o, s_fin


def gated_deltanet(h, s0, buf, w_in, conv_w, a_log, dt_bias, norm_g, w_out, chunk):
    bsz, t, _ = h.shape
    p = h @ w_in
    qkv = p[..., :DN_QKV_DIM]
    z = p[..., DN_QKV_DIM:DN_QKV_DIM + DN_V_DIM]
    b = p[..., DN_QKV_DIM + DN_V_DIM:DN_QKV_DIM + DN_V_DIM + DN_V_HEADS]
    a = p[..., DN_QKV_DIM + DN_V_DIM + DN_V_HEADS:]
    ext = jnp.concatenate([buf.astype(qkv.dtype), qkv], axis=1)
    c = jax.nn.silu(causal_dwconv(ext, conv_w)).astype(jnp.float32)
    q = c[..., :DN_QK_DIM].reshape(bsz, t, DN_QK_HEADS, DN_HEAD_DIM)
    k = c[..., DN_QK_DIM:2 * DN_QK_DIM].reshape(bsz, t, DN_QK_HEADS, DN_HEAD_DIM)
    v = c[..., 2 * DN_QK_DIM:].reshape(bsz, t, DN_V_HEADS, DN_HEAD_DIM)
    rep = DN_V_HEADS // DN_QK_HEADS
    q = jnp.repeat(l2norm(q), rep, axis=2) * (DN_HEAD_DIM ** -0.5)
    k = jnp.repeat(l2norm(k), rep, axis=2)
    beta = jax.nn.sigmoid(b.astype(jnp.float32))
    g = -jnp.exp(a_log.astype(jnp.float32)) * jax.nn.softplus(a.astype(jnp.float32) + dt_bias.astype(jnp.float32))
    o, s_new = gated_delta_rule(q, k, v, g, beta, s0.astype(jnp.float32), chunk)
    zf = z.astype(jnp.float32).reshape(bsz, t, DN_V_HEADS, DN_HEAD_DIM)
    o = o * lax.rsqrt(jnp.mean(o * o, axis=-1, keepdims=True) + EPS) * norm_g.astype(jnp.float32) * jax.nn.silu(zf)
    out = o.reshape(bsz, t, DN_V_DIM).astype(h.dtype) @ w_out
    return out, s_new, ext[:, -(DN_CONV_WIDTH - 1):]


def short_conv(h, buf, w_in, conv_w, w_out):
    p = h @ w_in
    bg, cg, xin = p[..., :D_MODEL], p[..., D_MODEL:2 * D_MODEL], p[..., 2 * D_MODEL:]
    ext = jnp.concatenate([buf.astype(p.dtype), cg * xin], axis=1)
    return (bg * causal_dwconv(ext, conv_w)) @ w_out, ext[:, -(SC_WIDTH - 1):]


def t5_bucket(rel):
    half = REL_BUCKETS // 2
    max_exact = half // 2
    a = jnp.abs(rel)
    af = jnp.maximum(a, 1).astype(jnp.float32)
    large = max_exact + (jnp.log(af / max_exact) / math.log(REL_MAX_DIST / max_exact)
                         * (half - max_exact)).astype(jnp.int32)
    large = jnp.minimum(large, half - 1)
    return jnp.where(rel > 0, half, 0) + jnp.where(a < max_exact, a, large)


def t5_bias(rel_bias, n_q, n_k):
    rel = jnp.arange(n_k)[None, :] - WINDOW - jnp.arange(n_q)[:, None]
    bias = jnp.take(rel_bias, t5_bucket(rel), axis=0).astype(jnp.float32)
    return jnp.transpose(bias, (2, 0, 1)).reshape(SWA_KV_HEADS, SWA_GROUP, n_q, n_k)


def sink_attention(q, k, v, bias, mask, sinks):
    s = jnp.einsum("...qngd,...knd->...ngqk", q, k).astype(jnp.float32) * (SWA_HEAD_DIM ** -0.5) + bias
    if mask is not None:
        s = jnp.where(mask, s, -jnp.inf)
    sink = sinks.astype(jnp.float32).reshape(SWA_KV_HEADS, SWA_GROUP, 1, 1)
    mx = jnp.maximum(jnp.max(s, axis=-1, keepdims=True), sink)
    p = jnp.exp(s - mx)
    denom = jnp.sum(p, axis=-1, keepdims=True) + jnp.exp(sink - mx)
    return jnp.einsum("...ngqk,...knd->...qngd", (p / denom).astype(v.dtype), v)


def swa_mixer(h, cache_k, cache_v, w_qkv, sinks, w_out, rel_bias):
    bsz, t, _ = h.shape
    qkv = h @ w_qkv
    q = qkv[..., :SWA_Q_DIM].reshape(bsz, t, SWA_KV_HEADS, SWA_GROUP, SWA_HEAD_DIM)
    k = qkv[..., SWA_Q_DIM:SWA_Q_DIM + SWA_KV_DIM].reshape(bsz, t, SWA_KV_HEADS, SWA_HEAD_DIM)
    v = qkv[..., SWA_Q_DIM + SWA_KV_DIM:].reshape(bsz, t, SWA_KV_HEADS, SWA_HEAD_DIM)
    if cache_k is None:
        nc, nw = t // CHUNK, WINDOW // CHUNK
        pad = jnp.zeros((bsz, WINDOW, SWA_KV_HEADS, SWA_HEAD_DIM), k.dtype)
        k_ext = jnp.concatenate([pad, k], axis=1)
        v_ext = jnp.concatenate([pad, v], axis=1)
        kp = k_ext.reshape(bsz, nc + nw, CHUNK, SWA_KV_HEADS, SWA_HEAD_DIM)
        vp = v_ext.reshape(bsz, nc + nw, CHUNK, SWA_KV_HEADS, SWA_HEAD_DIM)
        kb = jnp.concatenate([kp[:, j:j + nc] for j in range(nw + 1)], axis=2)
        vb = jnp.concatenate([vp[:, j:j + nc] for j in range(nw + 1)], axis=2)
        qb = q.reshape(bsz, nc, CHUNK, SWA_KV_HEADS, SWA_GROUP, SWA_HEAD_DIM)
        key_pos = jnp.arange(nc)[:, None] * CHUNK - WINDOW + jnp.arange(WINDOW + CHUNK)[None, :]
        mask = (key_pos >= 0)[:, None, None, None, :]
        o = sink_attention(qb, kb, vb, t5_bias(rel_bias, CHUNK, WINDOW + CHUNK), mask, sinks)
    else:
        k_ext = jnp.concatenate([cache_k.astype(k.dtype), k], axis=1)
        v_ext = jnp.concatenate([cache_v.astype(v.dtype), v], axis=1)
        o = sink_attention(q, k_ext, v_ext, t5_bias(rel_bias, t, WINDOW + t), None, sinks)
    o = o.reshape(bsz, t, SWA_Q_DIM)
    return o @ w_out, k_ext[:, -WINDOW:], v_ext[:, -WINDOW:]


def run_group(x, conv_a, delta_s, delta_conv, sconv, swa_k, swa_v, w, first_chunk):
    chunk = CHUNK if first_chunk else x.shape[1]
    new_conv_a, new_ds, new_dc, new_sc, new_k, new_v = [], [], [], [], [], []
    for i in range(DEPTH):
        mix, j = i % N_MIXERS, i // N_MIXERS
        h = rmsnorm(x, w["norm_mix_pre"][i])
        if mix == 0:
            out, buf = conformer_conv(h, conv_a[j], w["conv_a_w1"][j], w["conv_a_b1"][j], w["conv_a_dw"][j],
                                      w["conv_a_dw_b"][j], w["conv_a_ln_g"][j], w["conv_a_ln_b"][j],
                                      w["conv_a_w2"][j], w["conv_a_b2"][j])
            new_conv_a.append(buf)
        elif mix == 1:
            out, s_new, buf = gated_deltanet(h, delta_s[j], delta_conv[j], w["delta_w_in"][j], w["delta_conv_w"][j],
                                             w["delta_a_log"][j], w["delta_dt_bias"][j], w["delta_norm_g"][j],
                                             w["delta_w_out"][j], chunk)
            new_ds.append(s_new)
            new_dc.append(buf)
        elif mix == 2:
            out, buf = short_conv(h, sconv[j], w["sconv_w_in"][j], w["sconv_w"][j], w["sconv_w_out"][j])
            new_sc.append(buf)
        else:
            ck = None if first_chunk else swa_k[j]
            cv = None if first_chunk else swa_v[j]
            out, nk, nv = swa_mixer(h, ck, cv, w["swa_w_qkv"][j], w["swa_sinks"][j], w["swa_w_out"][j], w["rel_bias"])
            new_k.append(nk)
            new_v.append(nv)
        x = x + rmsnorm(out, w["norm_mix_post"][i])
        h = rmsnorm(x, w["norm_ffn_pre"][i])
        x = x + rmsnorm(swiglu(h, w["ffn_w_gate"][i], w["ffn_w_up"][i], w["ffn_w_down"][i]), w["norm_ffn_post"][i])
    return x, (jnp.stack(new_conv_a), jnp.stack(new_ds), jnp.stack(new_dc),
               jnp.stack(new_sc), jnp.stack(new_k), jnp.stack(new_v))


def setup_inputs(seed: int = 0) -> dict:
    key = jax.random.key(seed)
    ks = iter(jax.random.split(key, 48))

    def nrm(shape, scale=1.0):
        return jax.random.normal(next(ks), shape, jnp.float32) * scale

    def gain(shape):
        return 1.0 + nrm(shape, 0.05)

    d, f = D_MODEL, FFN_HIDDEN
    return {
        "x_prompt": nrm((BATCH, SEQ, d)),
        "x_sample": nrm((DEC_BATCH, DEC_SEQ, d)),
        "cache_conv_a": nrm((N_A, DEC_BATCH, CONV_A_WIDTH - 1, d)),
        "state_delta_s": nrm((N_B, DEC_BATCH, DN_V_HEADS, DN_HEAD_DIM, DN_HEAD_DIM), 0.05),
        "state_delta_conv": nrm((N_B, DEC_BATCH, DN_CONV_WIDTH - 1, DN_QKV_DIM)),
        "cache_sconv": nrm((N_C, DEC_BATCH, SC_WIDTH - 1, d)),
        "cache_swa_k": nrm((N_D, DEC_BATCH, WINDOW, SWA_KV_HEADS, SWA_HEAD_DIM)),
        "cache_swa_v": nrm((N_D, DEC_BATCH, WINDOW, SWA_KV_HEADS, SWA_HEAD_DIM)),
        "rel_bias": nrm((REL_BUCKETS, SWA_HEADS), 0.5),
        "norm_mix_pre": gain((DEPTH, d)),
        "norm_mix_post": gain((DEPTH, d)),
        "norm_ffn_pre": gain((DEPTH, d)),
        "norm_ffn_post": gain((DEPTH, d)),
        "ffn_w_gate": nrm((DEPTH, d, f), d ** -0.5),
        "ffn_w_up": nrm((DEPTH, d, f), d ** -0.5),
        "ffn_w_down": nrm((DEPTH, f, d), f ** -0.5),
        "conv_a_w1": nrm((N_A, d, 2 * d), d ** -0.5),
        "conv_a_b1": nrm((N_A, 2 * d), 0.02),
        "conv_a_dw": nrm((N_A, CONV_A_WIDTH, d), CONV_A_WIDTH ** -0.5),
        "conv_a_dw_b": nrm((N_A, d), 0.02),
        "conv_a_ln_g": gain((N_A, d)),
        "conv_a_ln_b": nrm((N_A, d), 0.02),
        "conv_a_w2": nrm((N_A, d, d), d ** -0.5),
        "conv_a_b2": nrm((N_A, d), 0.02),
        "delta_w_in": nrm((N_B, d, DN_IN_DIM), d ** -0.5),
        "delta_conv_w": nrm((N_B, DN_CONV_WIDTH, DN_QKV_DIM), DN_CONV_WIDTH ** -0.5),
        "delta_a_log": jnp.log(jax.random.uniform(next(ks), (N_B, DN_V_HEADS), jnp.float32, 1.0, 8.0)),
        "delta_dt_bias": jax.random.uniform(next(ks), (N_B, DN_V_HEADS), jnp.float32, -4.6, -2.3),
        "delta_norm_g": gain((N_B, DN_HEAD_DIM)),
        "delta_w_out": nrm((N_B, DN_V_DIM, d), DN_V_DIM ** -0.5),
        "sconv_w_in": nrm((N_C, d, 3 * d), d ** -0.5),
        "sconv_w": nrm((N_C, SC_WIDTH, d), SC_WIDTH ** -0.5),
        "sconv_w_out": nrm((N_C, d, d), d ** -0.5),
        "swa_w_qkv": nrm((N_D, d, SWA_Q_DIM + 2 * SWA_KV_DIM), d ** -0.5),
        "swa_sinks": nrm((N_D, SWA_HEADS), 0.5),
        "swa_w_out": nrm((N_D, SWA_Q_DIM, d), SWA_Q_DIM ** -0.5),
    }


def reference(x_prompt, x_sample, cache_conv_a, state_delta_s, state_delta_conv, cache_sconv,
              cache_swa_k, cache_swa_v, rel_bias, norm_mix_pre, norm_mix_post, norm_ffn_pre,
              norm_ffn_post, ffn_w_gate, ffn_w_up, ffn_w_down, conv_a_w1, conv_a_b1, conv_a_dw,
              conv_a_dw_b, conv_a_ln_g, conv_a_ln_b, conv_a_w2, conv_a_b2, delta_w_in, delta_conv_w,
              delta_a_log, delta_dt_bias, delta_norm_g, delta_w_out, sconv_w_in, sconv_w, sconv_w_out,
              swa_w_qkv, swa_sinks, swa_w_out):
    w = {
        "rel_bias": rel_bias, "norm_mix_pre": norm_mix_pre, "norm_mix_post": norm_mix_post,
        "norm_ffn_pre": norm_ffn_pre, "norm_ffn_post": norm_ffn_post, "ffn_w_gate": ffn_w_gate,
        "ffn_w_up": ffn_w_up, "ffn_w_down": ffn_w_down, "conv_a_w1": conv_a_w1, "conv_a_b1": conv_a_b1,
        "conv_a_dw": conv_a_dw, "conv_a_dw_b": conv_a_dw_b, "conv_a_ln_g": conv_a_ln_g,
        "conv_a_ln_b": conv_a_ln_b, "conv_a_w2": conv_a_w2, "conv_a_b2": conv_a_b2,
        "delta_w_in": delta_w_in, "delta_conv_w": delta_conv_w, "delta_a_log": delta_a_log,
        "delta_dt_bias": delta_dt_bias, "delta_norm_g": delta_norm_g, "delta_w_out": delta_w_out,
        "sconv_w_in": sconv_w_in, "sconv_w": sconv_w, "sconv_w_out": sconv_w_out,
        "swa_w_qkv": swa_w_qkv, "swa_sinks": swa_sinks, "swa_w_out": swa_w_out,
    }
    bsz = x_prompt.shape[0]
    dt = x_prompt.dtype
    conv_a0 = jnp.zeros((N_A, bsz, CONV_A_WIDTH - 1, D_MODEL), dt)
    delta_s0 = jnp.zeros((N_B, bsz, DN_V_HEADS, DN_HEAD_DIM, DN_HEAD_DIM), jnp.float32)
    delta_conv0 = jnp.zeros((N_B, bsz, DN_CONV_WIDTH - 1, DN_QKV_DIM), dt)
    sconv0 = jnp.zeros((N_C, bsz, SC_WIDTH - 1, D_MODEL), dt)
    y_prompt, (ca_p, ds_p, dc_p, sc_p, k_p, v_p) = run_group(
        x_prompt, conv_a0, delta_s0, delta_conv0, sconv0, None, None, w, True)
    y_sample, (ca_s, ds_s, dc_s, sc_s, k_s, v_s) = run_group(
        x_sample, cache_conv_a, state_delta_s, state_delta_conv, cache_sconv, cache_swa_k, cache_swa_v, w, False)
    return (y_prompt, y_sample, ca_p, ca_s, ds_p, ds_s, dc_p, dc_s, sc_p, sc_s, k_p, k_s, v_p, v_s)
```

```python
import functools
import math

import jax
import jax.numpy as jnp
from jax import lax
from jax.experimental import pallas as pl
from jax.experimental.pallas import tpu as pltpu

F32 = jnp.float32
BF16 = jnp.bfloat16
EPS = 1e-6
CHUNK = 64
WINDOW = 128
SWA_HEAD_DIM = 64
DN_HEAD_DIM = 128
REL_BUCKETS = 32
REL_MAX_DIST = 128
V7X_VMEM_BUDGET = 56 * 1024 * 1024
SUBLANES = 8
LANES = 128
NEG_INF = float("-inf")


def _cparams(sem):
    return pltpu.CompilerParams(dimension_semantics=sem, vmem_limit_bytes=V7X_VMEM_BUDGET)


def _rms(x, g):
    return x * lax.rsqrt(jnp.mean(x * x, axis=-1, keepdims=True) + EPS) * g


def _silu(x):
    return x * jax.nn.sigmoid(x)


def _mm(a, b):
    return jnp.dot(a.astype(BF16), b.astype(BF16), preferred_element_type=F32)


def _mm_nt(a, b):
    return lax.dot_general(a.astype(BF16), b.astype(BF16), (((1,), (1,)), ((), ())),
                           preferred_element_type=F32)


def _mm_tn(a, b):
    return lax.dot_general(a.astype(BF16), b.astype(BF16), (((0,), (0,)), ((), ())),
                           preferred_element_type=F32)


def _mm_f32(a, b):
    return jnp.dot(a, b, preferred_element_type=F32, precision=lax.Precision.HIGHEST)


def _pick_tile(n, pref):
    t = min(n, pref)
    while n % t:
        t //= 2
    return t


def _norm_mm_kernel(*refs, n_w, has_bias, combine, n_out):
    x_ref, g_ref = refs[0], refs[1]
    w_refs = refs[2:2 + n_w]
    pos = 2 + n_w
    b_refs = refs[pos:pos + n_w] if has_bias else ()
    pos += n_w if has_bias else 0
    o_refs = refs[pos:pos + n_out]
    h_ref = refs[pos + n_out]

    @pl.when(pl.program_id(1) == 0)
    def _():
        h_ref[...] = _rms(x_ref[...], g_ref[...]).astype(BF16)

    h = h_ref[...]
    ds = []
    for i in range(n_w):
        d = jnp.dot(h, w_refs[i][...], preferred_element_type=F32)
        if has_bias:
            d = d + b_refs[i][...]
        ds.append(d)
    outs = combine(*ds)
    for o_ref, o in zip(o_refs, outs):
        o_ref[...] = o.astype(o_ref.dtype)


def norm_matmul(x, g, ws, biases, combine, out_dtypes, tm=512, tn=512):
    t, k = x.shape
    n = ws[0].shape[1]
    tm = _pick_tile(t, tm)
    tn = _pick_tile(n, tn)
    n_w, n_out = len(ws), len(out_dtypes)
    has_bias = biases is not None
    in_specs = [pl.BlockSpec((tm, k), lambda i, j: (i, 0)),
                pl.BlockSpec((1, k), lambda i, j: (0, 0))]
    in_specs += [pl.BlockSpec((k, tn), lambda i, j: (0, j)) for _ in ws]
    args = [x, g.reshape(1, k)] + list(ws)
    if has_bias:
        in_specs += [pl.BlockSpec((1, tn), lambda i, j: (0, j)) for _ in ws]
        args += [b.reshape(1, n) for b in biases]
    outs = pl.pallas_call(
        functools.partial(_norm_mm_kernel, n_w=n_w, has_bias=has_bias, combine=combine, n_out=n_out),
        grid=(t // tm, n // tn),
        in_specs=in_specs,
        out_specs=[pl.BlockSpec((tm, tn), lambda i, j: (i, j)) for _ in out_dtypes],
        out_shape=[jax.ShapeDtypeStruct((t, n), dt) for dt in out_dtypes],
        scratch_shapes=[pltpu.VMEM((tm, k), BF16)],
        compiler_params=_cparams(("parallel", "arbitrary")),
    )(*args)
    return outs


def _comb_id(d):
    return (d,)


def _comb_glu(a, b):
    return (a * jax.nn.sigmoid(b),)


def _comb_sconv(bg, cg, xin):
    return (bg, cg * xin)


def _mm_norm_res_kernel(a_ref, w_ref, g_ref, x_ref, o_ref):
    d = jnp.dot(a_ref[...], w_ref[...], preferred_element_type=F32)
    o_ref[...] = x_ref[...] + _rms(d, g_ref[...])


def matmul_norm_residual(a, w, g, x, tm=512):
    t, k = a.shape
    d = w.shape[1]
    tm = _pick_tile(t, tm)
    return pl.pallas_call(
        _mm_norm_res_kernel,
        grid=(t // tm,),
        in_specs=[pl.BlockSpec((tm, k), lambda i: (i, 0)),
                  pl.BlockSpec((k, d), lambda i: (0, 0), pipeline_mode=pl.Buffered(1)),
                  pl.BlockSpec((1, d), lambda i: (0, 0)),
                  pl.BlockSpec((tm, d), lambda i: (i, 0))],
        out_specs=pl.BlockSpec((tm, d), lambda i: (i, 0)),
        out_shape=jax.ShapeDtypeStruct((t, d), F32),
        compiler_params=_cparams(("parallel",)),
    )(a, w, g.reshape(1, d), x)


def _ffn_kernel(x_ref, g1_ref, wg_ref, wu_ref, wd_ref, g2_ref, o_ref, h_ref, acc_ref, *, nf):
    f = pl.program_id(1)

    @pl.when(f == 0)
    def _():
        h_ref[...] = _rms(x_ref[...], g1_ref[...]).astype(BF16)

    h = h_ref[...]
    gate = jnp.dot(h, wg_ref[...], preferred_element_type=F32)
    up = jnp.dot(h, wu_ref[...], preferred_element_type=F32)
    a = (_silu(gate) * up).astype(BF16)
    d = jnp.dot(a, wd_ref[...], preferred_element_type=F32)

    @pl.when(f == 0)
    def _():
        acc_ref[...] = d

    @pl.when(f > 0)
    def _():
        acc_ref[...] += d

    @pl.when(f == nf - 1)
    def _():
        o_ref[...] = x_ref[...] + _rms(acc_ref[...], g2_ref[...])


def ffn(x, g1, wg, wu, wd, g2, tm=512, tf=512):
    t, d = x.shape
    fh = wg.shape[1]
    tm = _pick_tile(t, tm)
    tf = _pick_tile(fh, tf)
    nf = fh // tf
    return pl.pallas_call(
        functools.partial(_ffn_kernel, nf=nf),
        grid=(t // tm, nf),
        in_specs=[pl.BlockSpec((tm, d), lambda i, f: (i, 0)),
                  pl.BlockSpec((1, d), lambda i, f: (0, 0)),
                  pl.BlockSpec((d, tf), lambda i, f: (0, f)),
                  pl.BlockSpec((d, tf), lambda i, f: (0, f)),
                  pl.BlockSpec((tf, d), lambda i, f: (f, 0)),
                  pl.BlockSpec((1, d), lambda i, f: (0, 0))],
        out_specs=pl.BlockSpec((tm, d), lambda i, f: (i, 0)),
        out_shape=jax.ShapeDtypeStruct((t, d), F32),
        scratch_shapes=[pltpu.VMEM((tm, d), BF16), pltpu.VMEM((tm, d), F32)],
        compiler_params=_cparams(("parallel", "arbitrary")),
    )(x, g1.reshape(1, d), wg, wu, wd, g2.reshape(1, d))


def _fill_ext(ext_ref, cur_ref, prev_ref, cache_ref, hb, tt):
    first = pl.program_id(1) == 0

    @pl.when(first)
    def _():
        ext_ref[0:hb, :] = cache_ref[0]

    @pl.when(jnp.logical_not(first))
    def _():
        ext_ref[0:hb, :] = prev_ref[0]

    ext_ref[hb:hb + tt, :] = cur_ref[0]


def _dwconv(ext_ref, w_ref, width, hb, tt, d, rows=64, cols=256):
    rows = min(rows, tt)
    cols = min(cols, d)
    base = hb - (width - 1)
    out_rows = []
    for r0 in range(0, tt, rows):
        out_cols = []
        for c0 in range(0, d, cols):
            acc = None
            for k in range(width):
                term = ext_ref[base + r0 + k:base + r0 + k + rows, c0:c0 + cols] * w_ref[k:k + 1, c0:c0 + cols]
                acc = term if acc is None else acc + term
            out_cols.append(acc)
        out_rows.append(jnp.concatenate(out_cols, axis=1))
    return jnp.concatenate(out_rows, axis=0)


def _conformer_tail_kernel(u_ref, prev_ref, cache_ref, dw_ref, dwb_ref, lng_ref, lnb_ref, w2_ref, b2_ref,
                           gpost_ref, x_ref, o_ref, ext_ref, *, width, hb, tt, d):
    _fill_ext(ext_ref, u_ref, prev_ref, cache_ref, hb, tt)
    c = _dwconv(ext_ref, dw_ref, width, hb, tt, d) + dwb_ref[...]
    cc = c - jnp.mean(c, axis=-1, keepdims=True)
    y = cc * lax.rsqrt(jnp.mean(cc * cc, axis=-1, keepdims=True) + EPS) * lng_ref[...] + lnb_ref[...]
    out = jnp.dot(_silu(y).astype(BF16), w2_ref[...], preferred_element_type=F32) + b2_ref[...]
    o_ref[0] = x_ref[0] + _rms(out, gpost_ref[...])


def _sconv_tail_kernel(gx_ref, prev_ref, cache_ref, bg_ref, cw_ref, w_ref, gpost_ref, x_ref, o_ref, ext_ref,
                       *, width, hb, tt, d):
    _fill_ext(ext_ref, gx_ref, prev_ref, cache_ref, hb, tt)
    c = _dwconv(ext_ref, cw_ref, width, hb, tt, d)
    out = jnp.dot((bg_ref[0] * c).astype(BF16), w_ref[...], preferred_element_type=F32)
    o_ref[0] = x_ref[0] + _rms(out, gpost_ref[...])


def _pad_rows(a, rows):
    pad = rows - a.shape[-2]
    cfg = [(0, 0)] * a.ndim
    cfg[-2] = (pad, 0)
    return jnp.pad(a, cfg)


def _halo_specs(tt, hb, d):
    step = tt // hb
    cur = pl.BlockSpec((1, tt, d), lambda b, i: (b, i, 0))
    prev = pl.BlockSpec((1, hb, d), lambda b, i: (b, jnp.maximum(i * step - 1, 0), 0))
    cache = pl.BlockSpec((1, hb, d), lambda b, i: (b, 0, 0))
    return cur, prev, cache


def _const_spec(shape, single=False):
    nd = len(shape)
    if single:
        return pl.BlockSpec(shape, lambda b, i: (0,) * nd, pipeline_mode=pl.Buffered(1))
    return pl.BlockSpec(shape, lambda b, i: (0,) * nd)


def conformer_tail(u, cache, dw, dwb, lng, lnb, w2, b2, gpost, x, tt=256):
    bsz, t, d = u.shape
    width = dw.shape[0]
    hb = 32
    tt = _pick_tile(t, tt)
    cur, prev, cch = _halo_specs(tt, hb, d)
    row = lambda a: a.reshape(1, d)
    return pl.pallas_call(
        functools.partial(_conformer_tail_kernel, width=width, hb=hb, tt=tt, d=d),
        grid=(bsz, t // tt),
        in_specs=[cur, prev, cch, _const_spec((hb, d)), _const_spec((1, d)), _const_spec((1, d)),
                  _const_spec((1, d)), _const_spec((d, d), single=True), _const_spec((1, d)),
                  _const_spec((1, d)), cur],
        out_specs=pl.BlockSpec((1, tt, d), lambda b, i: (b, i, 0)),
        out_shape=jax.ShapeDtypeStruct((bsz, t, d), F32),
        scratch_shapes=[pltpu.VMEM((hb + tt, d), F32)],
        compiler_params=_cparams(("parallel", "arbitrary")),
    )(u, u, _pad_rows(cache, hb), jnp.pad(dw, ((0, hb - width), (0, 0))), row(dwb), row(lng), row(lnb),
      w2, row(b2), row(gpost), x)


def sconv_tail(gx, cache, bg, cw, w_out, gpost, x, tt=256):
    bsz, t, d = gx.shape
    width = cw.shape[0]
    hb = SUBLANES
    tt = _pick_tile(t, tt)
    cur, prev, cch = _halo_specs(tt, hb, d)
    return pl.pallas_call(
        functools.partial(_sconv_tail_kernel, width=width, hb=hb, tt=tt, d=d),
        grid=(bsz, t // tt),
        in_specs=[cur, prev, cch, cur, _const_spec((hb, d)), _const_spec((d, d), single=True),
                  _const_spec((1, d)), cur],
        out_specs=pl.BlockSpec((1, tt, d), lambda b, i: (b, i, 0)),
        out_shape=jax.ShapeDtypeStruct((bsz, t, d), F32),
        scratch_shapes=[pltpu.VMEM((hb + tt, d), F32)],
        compiler_params=_cparams(("parallel", "arbitrary")),
    )(gx, gx, _pad_rows(cache, hb), bg, jnp.pad(cw, ((0, hb - width), (0, 0))), w_out, gpost.reshape(1, d), x)


def _tri_inverse(m, n):
    r = lax.broadcasted_iota(jnp.int32, (n, n), 0)
    c = lax.broadcasted_iota(jnp.int32, (n, n), 1)
    t = jnp.where(r == c, 1.0, 0.0).astype(F32) - m
    p = m
    for _ in range(int(math.log2(n)) - 1):
        p = _mm_f32(p, p)
        t = t + _mm_f32(t, p)
    return t


def _delta_kernel(*refs, hp, L, has_state):
    nh = 2 * hp
    hd = DN_HEAD_DIM
    if has_state:
        (q_ref, k_ref, v_ref, z_ref, ba_ref, cq_ref, ck_ref, cv_ref, wq_ref, wk_ref, wv_ref, par_ref, ng_ref,
         s0_ref, o_ref, s_ref, eq_ref, ek_ref, ev_ref) = refs
    else:
        (q_ref, k_ref, v_ref, z_ref, ba_ref, cq_ref, ck_ref, cv_ref, wq_ref, wk_ref, wv_ref, par_ref, ng_ref,
         o_ref, s_ref, eq_ref, ek_ref, ev_ref) = refs
        s0_ref = None
    hb = SUBLANES
    width = 4

    @pl.when(pl.program_id(2) == 0)
    def _():
        eq_ref[0:hb, :] = cq_ref[0]
        ek_ref[0:hb, :] = ck_ref[0]
        ev_ref[0:hb, :] = cv_ref[0]
        if has_state:
            s_ref[...] = s0_ref[...]
        else:
            s_ref[...] = jnp.zeros(s_ref.shape, F32)

    def conv_silu(e_ref, cur_ref, w_ref):
        e_ref[hb:hb + L, :] = cur_ref[0]
        acc = None
        for kk in range(width):
            term = e_ref[hb - (width - 1) + kk:hb - (width - 1) + kk + L, :] * w_ref[kk:kk + 1, :]
            acc = term if acc is None else acc + term
        e_ref[0:hb, :] = e_ref[L:L + hb, :]
        return _silu(acc)

    q_all = conv_silu(eq_ref, q_ref, wq_ref)
    k_all = conv_silu(ek_ref, k_ref, wk_ref)
    v_all = conv_silu(ev_ref, v_ref, wv_ref)

    ba = ba_ref[0]
    beta_all = jax.nn.sigmoid(ba)
    g_all = -jnp.exp(par_ref[0, 0:1, :]) * jax.nn.softplus(ba + par_ref[0, 1:2, :])
    ri = lax.broadcasted_iota(jnp.int32, (L, L), 0)
    ci = lax.broadcasted_iota(jnp.int32, (L, L), 1)
    incl = ri >= ci
    strict = ri > ci
    gc_all = _mm_f32(jnp.where(incl, 1.0, 0.0).astype(F32), g_all)
    gc_t = gc_all.T

    for j in range(hp):
        qj = q_all[:, j * hd:(j + 1) * hd]
        kj = k_all[:, j * hd:(j + 1) * hd]
        qn = qj * lax.rsqrt(jnp.sum(qj * qj, axis=-1, keepdims=True) + EPS) * (hd ** -0.5)
        kn = kj * lax.rsqrt(jnp.sum(kj * kj, axis=-1, keepdims=True) + EPS)
        kk_raw = _mm_nt(kn, kn)
        qk_raw = _mm_nt(qn, kn)
        for r in range(2):
            h = 2 * j + r
            vh = v_all[:, h * hd:(h + 1) * hd]
            gcol = gc_all[:, 64 + h:65 + h]
            grow = gc_t[64 + h:65 + h, :]
            bcol = beta_all[:, h:h + 1]
            glast = gc_all[L - 1:L, 64 + h:65 + h]
            decay = jnp.exp(jnp.where(incl, gcol - grow, NEG_INF))
            m = jnp.where(strict, bcol * kk_raw * decay, 0.0)
            tinv = _tri_inverse(m, L)
            egc = jnp.exp(gcol)
            rhs = jnp.concatenate([bcol * vh, (bcol * egc) * kn], axis=-1)
            sol = _mm(tinv, rhs)
            u, w = sol[:, :hd], sol[:, hd:]
            qkm = jnp.where(incl, qk_raw * decay, 0.0)
            qg = qn * egc
            kg = kn * jnp.exp(glast - gcol)
            s = s_ref[0, h]
            v_new = u - _mm(w, s)
            o = _mm(qg, s) + _mm(qkm, v_new)
            s_ref[0, h] = s * jnp.exp(glast) + _mm_tn(kg, v_new)
            zh = z_ref[0, :, h * hd:(h + 1) * hd]
            o = o * lax.rsqrt(jnp.mean(o * o, axis=-1, keepdims=True) + EPS) * ng_ref[...] * _silu(zh)
            o_ref[0, :, h * hd:(h + 1) * hd] = o.astype(o_ref.dtype)


def delta_rule(qkv, z, ba, conv_cache, conv_w, par, norm_g, s0, hp):
    bsz, t, _ = qkv.shape
    vdim = z.shape[-1]
    hd = DN_HEAD_DIM
    vh = vdim // hd
    qh = vh // 2
    hg = qh // hp
    nh = 2 * hp
    L = CHUNK
    nc = t // L
    hb = SUBLANES
    cache = _pad_rows(conv_cache, hb)
    cw = jnp.pad(conv_w, ((0, hb - conv_w.shape[0]), (0, 0)))
    qw, vw = hp * hd, nh * hd
    koff, voff = qh // hp, 2 * qh // nh
    in_specs = [
        pl.BlockSpec((1, L, qw), lambda b, g, c: (b, c, g)),
        pl.BlockSpec((1, L, qw), lambda b, g, c: (b, c, koff + g)),
        pl.BlockSpec((1, L, vw), lambda b, g, c: (b, c, voff + g)),
        pl.BlockSpec((1, L, vw), lambda b, g, c: (b, c, g)),
        pl.BlockSpec((1, L, LANES), lambda b, g, c: (b, c, g)),
        pl.BlockSpec((1, hb, qw), lambda b, g, c: (b, 0, g)),
        pl.BlockSpec((1, hb, qw), lambda b, g, c: (b, 0, koff + g)),
        pl.BlockSpec((1, hb, vw), lambda b, g, c: (b, 0, voff + g)),
        pl.BlockSpec((hb, qw), lambda b, g, c: (0, g)),
        pl.BlockSpec((hb, qw), lambda b, g, c: (0, koff + g)),
        pl.BlockSpec((hb, vw), lambda b, g, c: (0, voff + g)),
        pl.BlockSpec((1, hb, LANES), lambda b, g, c: (g, 0, 0)),
        pl.BlockSpec((1, hd), lambda b, g, c: (0, 0)),
    ]
    args = [qkv, qkv, qkv, z, ba, cache, cache, cache, cw, cw, cw, par, norm_g.reshape(1, hd)]
    if s0 is not None:
        in_specs.append(pl.BlockSpec((1, nh, hd, hd), lambda b, g, c: (b, g, 0, 0)))
        args.append(s0)
    o, s = pl.pallas_call(
        functools.partial(_delta_kernel, hp=hp, L=L, has_state=s0 is not None),
        grid=(bsz, hg, nc),
        in_specs=in_specs,
        out_specs=[pl.BlockSpec((1, L, vw), lambda b, g, c: (b, c, g)),
                   pl.BlockSpec((1, nh, hd, hd), lambda b, g, c: (b, g, 0, 0))],
        out_shape=[jax.ShapeDtypeStruct((bsz, t, vdim), BF16),
                   jax.ShapeDtypeStruct((bsz, vh, hd, hd), F32)],
        scratch_shapes=[pltpu.VMEM((hb + L, qw), F32), pltpu.VMEM((hb + L, qw), F32),
                        pltpu.VMEM((hb + L, vw), F32)],
        compiler_params=_cparams(("parallel", "parallel", "arbitrary")),
    )(*args)
    return o, s


def _attn_kernel(q_ref, k0_ref, k1_ref, k2_ref, v0_ref, v1_ref, v2_ref, bias_ref, sink_ref, o_ref,
                 *, nkv, grp, masked):
    hd = SWA_HEAD_DIM
    c = pl.program_id(1)
    q = q_ref[0]
    k = jnp.concatenate([k0_ref[0], k1_ref[0], k2_ref[0]], axis=0)
    v = jnp.concatenate([v0_ref[0], v1_ref[0], v2_ref[0]], axis=0)
    nq, nk = q.shape[0], k.shape[0]
    if masked:
        kpos = c * CHUNK - WINDOW + lax.broadcasted_iota(jnp.int32, (grp * nq, nk), 1)
        valid = kpos >= 0
    for n in range(nkv):
        kn = k[:, n * hd:(n + 1) * hd]
        vn = v[:, n * hd:(n + 1) * hd]
        qn = jnp.concatenate([q[:, (n * grp + g) * hd:(n * grp + g + 1) * hd] for g in range(grp)], axis=0)
        s = _mm_nt(qn, kn) * (hd ** -0.5) + bias_ref[n]
        if masked:
            s = jnp.where(valid, s, NEG_INF)
        sink = sink_ref[n]
        mx = jnp.maximum(jnp.max(s, axis=-1, keepdims=True), sink)
        p = jnp.exp(s - mx)
        denom = jnp.sum(p, axis=-1, keepdims=True) + jnp.exp(sink - mx)
        on = _mm(p / denom, vn)
        for g in range(grp):
            o_ref[0, :, (n * grp + g) * hd:(n * grp + g + 1) * hd] = on[g * nq:(g + 1) * nq].astype(o_ref.dtype)


def _t5_bucket(rel):
    half = REL_BUCKETS // 2
    max_exact = half // 2
    a = jnp.abs(rel)
    af = jnp.maximum(a, 1).astype(F32)
    large = max_exact + (jnp.log(af / max_exact) / math.log(REL_MAX_DIST / max_exact)
                         * (half - max_exact)).astype(jnp.int32)
    large = jnp.minimum(large, half - 1)
    return jnp.where(rel > 0, half, 0) + jnp.where(a < max_exact, a, large)


def _bias_table(rel_bias, nkv, grp, n_q, n_k):
    rel = jnp.arange(n_k)[None, :] - WINDOW - jnp.arange(n_q)[:, None]
    bias = jnp.take(rel_bias, _t5_bucket(rel), axis=0).astype(F32)
    return jnp.transpose(bias, (2, 0, 1)).reshape(nkv, grp * n_q, n_k)


def swa_attention(q_src, k_src, v_src, kv_col, rel_bias, sinks, nkv, masked):
    hd = SWA_HEAD_DIM
    nheads = sinks.shape[0]
    grp = nheads // nkv
    qd, kvd = nheads * hd, nkv * hd
    bsz = q_src.shape[0]
    t = k_src.shape[1] if masked else k_src.shape[1] - WINDOW
    nc = t // CHUNK
    bias = _bias_table(rel_bias, nkv, grp, CHUNK, WINDOW + CHUNK)
    sink = jnp.repeat(sinks.astype(F32).reshape(nkv, grp), CHUNK, axis=1).reshape(nkv, grp * CHUNK, 1)
    if masked:
        rows = [lambda b, c, j=j: jnp.maximum(c + j - 2, 0) for j in range(3)]
    else:
        rows = [lambda b, c, j=j: c + j for j in range(3)]
    kspecs = [pl.BlockSpec((1, CHUNK, kvd), lambda b, c, r=r: (b, r(b, c), kv_col[0])) for r in rows]
    vspecs = [pl.BlockSpec((1, CHUNK, kvd), lambda b, c, r=r: (b, r(b, c), kv_col[1])) for r in rows]
    return pl.pallas_call(
        functools.partial(_attn_kernel, nkv=nkv, grp=grp, masked=masked),
        grid=(bsz, nc),
        in_specs=[pl.BlockSpec((1, CHUNK, qd), lambda b, c: (b, c, 0))] + kspecs + vspecs
        + [pl.BlockSpec((nkv, grp * CHUNK, WINDOW + CHUNK), lambda b, c: (0, 0, 0)),
           pl.BlockSpec((nkv, grp * CHUNK, 1), lambda b, c: (0, 0, 0))],
        out_specs=pl.BlockSpec((1, CHUNK, qd), lambda b, c: (b, c, 0)),
        out_shape=jax.ShapeDtypeStruct((bsz, t, qd), BF16),
        compiler_params=_cparams(("parallel", "arbitrary")),
    )(q_src, k_src, k_src, k_src, v_src, v_src, v_src, bias, sink)


def _delta_heads_per_step(qh):
    return min(4, qh)


def _run_group(x, caches, w, first_chunk):
    conv_a, delta_s, delta_conv, sconv, swa_k, swa_v = caches
    bsz, t, d = x.shape
    depth = w["norm_mix_pre"].shape[0]
    xf = x.reshape(bsz * t, d)
    new = {k: [] for k in ("conv_a", "ds", "dc", "sc", "k", "v")}
    bf = lambda a: a.astype(BF16)
    for i in range(depth):
        mix, j = i % 4, i // 4
        g_pre, g_post = w["norm_mix_pre"][i], w["norm_mix_post"][i]
        if mix == 0:
            w1 = bf(w["conv_a_w1"][j])
            b1 = w["conv_a_b1"][j]
            (u,) = norm_matmul(xf, g_pre, [w1[:, :d], w1[:, d:]], [b1[:d], b1[d:]], _comb_glu, [F32])
            u = u.reshape(bsz, t, d)
            width = w["conv_a_dw"].shape[1]
            cache = conv_a[j] if conv_a is not None else jnp.zeros((bsz, width - 1, d), F32)
            xf = conformer_tail(u, cache, w["conv_a_dw"][j], w["conv_a_dw_b"][j], w["conv_a_ln_g"][j],
                                w["conv_a_ln_b"][j], bf(w["conv_a_w2"][j]), w["conv_a_b2"][j], g_post,
                                xf.reshape(bsz, t, d)).reshape(bsz * t, d)
            new["conv_a"].append(jnp.concatenate([cache, u], axis=1)[:, -(width - 1):] if t < width - 1
                                 else u[:, -(width - 1):])
        elif mix == 1:
            w_in = w["delta_w_in"][j]
            vh = w["delta_a_log"].shape[1]
            vdim = vh * DN_HEAD_DIM
            qkvd = w["delta_conv_w"].shape[2]
            qh = (qkvd - vdim) // (2 * DN_HEAD_DIM)
            hp = _delta_heads_per_step(qh)
            nh, hg = 2 * hp, qh // hp
            (qkv,) = norm_matmul(xf, g_pre, [bf(w_in[:, :qkvd])], None, _comb_id, [F32], tn=1024)
            (z,) = norm_matmul(xf, g_pre, [bf(w_in[:, qkvd:qkvd + vdim])], None, _comb_id, [F32], tn=1024)
            wb = w_in[:, qkvd + vdim:qkvd + vdim + vh].reshape(d, hg, nh)
            wa = w_in[:, qkvd + vdim + vh:].reshape(d, hg, nh)
            zpad = jnp.zeros((d, hg, 64 - nh), F32)
            w_ba = jnp.concatenate([wb, zpad, wa, zpad], axis=2).reshape(d, hg * LANES)
            (ba,) = norm_matmul(xf, g_pre, [bf(w_ba)], None, _comb_id, [F32], tn=LANES)
            pz = jnp.zeros((hg, 64 - nh), F32)
            p64 = jnp.zeros((hg, 64), F32)
            par = jnp.stack([jnp.concatenate([p64, w["delta_a_log"][j].reshape(hg, nh), pz], axis=1),
                             jnp.concatenate([p64, w["delta_dt_bias"][j].reshape(hg, nh), pz], axis=1)], axis=1)
            par = jnp.pad(par, ((0, 0), (0, SUBLANES - 2), (0, 0)))
            cw = w["delta_conv_w"][j]
            cache = delta_conv[j] if delta_conv is not None else jnp.zeros((bsz, cw.shape[0] - 1, qkvd), F32)
            qkv3 = qkv.reshape(bsz, t, qkvd)
            o, s_new = delta_rule(qkv3, z.reshape(bsz, t, vdim), ba.reshape(bsz, t, hg * LANES), cache, cw, par,
                                  w["delta_norm_g"][j], delta_s[j] if delta_s is not None else None, hp)
            xf = matmul_norm_residual(o.reshape(bsz * t, vdim), bf(w["delta_w_out"][j]), g_post, xf)
            new["ds"].append(s_new)
            new["dc"].append(qkv3[:, -(cw.shape[0] - 1):])
        elif mix == 2:
            w_in = bf(w["sconv_w_in"][j])
            bg, gx = norm_matmul(xf, g_pre, [w_in[:, :d], w_in[:, d:2 * d], w_in[:, 2 * d:]], None, _comb_sconv,
                                 [F32, F32])
            gx = gx.reshape(bsz, t, d)
            cw = w["sconv_w"][j]
            cache = sconv[j] if sconv is not None else jnp.zeros((bsz, cw.shape[0] - 1, d), F32)
            xf = sconv_tail(gx, cache, bg.reshape(bsz, t, d), cw, bf(w["sconv_w_out"][j]), g_post,
                            xf.reshape(bsz, t, d)).reshape(bsz * t, d)
            new["sc"].append(gx[:, -(cw.shape[0] - 1):])
        else:
            w_qkv = w["swa_w_qkv"][j]
            nheads = w["swa_sinks"].shape[1]
            qd = nheads * SWA_HEAD_DIM
            kvd = (w_qkv.shape[1] - qd) // 2
            nkv = kvd // SWA_HEAD_DIM
            (qkv,) = norm_matmul(xf, g_pre, [bf(w_qkv)], None, _comb_id, [F32])
            qkv = qkv.reshape(bsz, t, qd + 2 * kvd)
            if first_chunk:
                o = swa_attention(qkv, qkv, qkv, (qd // kvd, qd // kvd + 1), w["rel_bias"], w["swa_sinks"][j],
                                  nkv, True)
                k_ext, v_ext = qkv[:, :, qd:qd + kvd], qkv[:, :, qd + kvd:]
            else:
                k_ext = jnp.concatenate([swa_k[j].reshape(bsz, WINDOW, kvd), qkv[:, :, qd:qd + kvd]], axis=1)
                v_ext = jnp.concatenate([swa_v[j].reshape(bsz, WINDOW, kvd), qkv[:, :, qd + kvd:]], axis=1)
                o = swa_attention(qkv, k_ext, v_ext, (0, 0), w["rel_bias"], w["swa_sinks"][j], nkv, False)
            xf = matmul_norm_residual(o.reshape(bsz * t, qd), bf(w["swa_w_out"][j]), g_post, xf)
            new["k"].append(k_ext[:, -WINDOW:].reshape(bsz, WINDOW, nkv, SWA_HEAD_DIM))
            new["v"].append(v_ext[:, -WINDOW:].reshape(bsz, WINDOW, nkv, SWA_HEAD_DIM))
        xf = ffn(xf, w["norm_ffn_pre"][i], bf(w["ffn_w_gate"][i]), bf(w["ffn_w_up"][i]), bf(w["ffn_w_down"][i]),
                 w["norm_ffn_post"][i])
    return xf.reshape(bsz, t, d), tuple(jnp.stack(new[k]) for k in ("conv_a", "ds", "dc", "sc", "k", "v"))


def kernel(x_prompt, x_sample, cache_conv_a, state_delta_s, state_delta_conv, cache_sconv, cache_swa_k, cache_swa_v, rel_bias, norm_mix_pre, norm_mix_post, norm_ffn_pre, norm_ffn_post, ffn_w_gate, ffn_w_up, ffn_w_down, conv_a_w1, conv_a_b1, conv_a_dw, conv_a_dw_b, conv_a_ln_g, conv_a_ln_b, conv_a_w2, conv_a_b2, delta_w_in, delta_conv_w, delta_a_log, delta_dt_bias, delta_norm_g, delta_w_out, sconv_w_in, sconv_w, sconv_w_out, swa_w_qkv, swa_sinks, swa_w_out):
    w = {
        "rel_bias": rel_bias, "norm_mix_pre": norm_mix_pre, "norm_mix_post": norm_mix_post,
        "norm_ffn_pre": norm_ffn_pre, "norm_ffn_post": norm_ffn_post, "ffn_w_gate": ffn_w_gate,
        "ffn_w_up": ffn_w_up, "ffn_w_down": ffn_w_down, "conv_a_w1": conv_a_w1, "conv_a_b1": conv_a_b1,
        "conv_a_dw": conv_a_dw, "conv_a_dw_b": conv_a_dw_b, "conv_a_ln_g": conv_a_ln_g,
        "conv_a_ln_b": conv_a_ln_b, "conv_a_w2": conv_a_w2, "conv_a_b2": conv_a_b2,
        "delta_w_in": delta_w_in, "delta_conv_w": delta_conv_w, "delta_a_log": delta_a_log,
        "delta_dt_bias": delta_dt_bias, "delta_norm_g": delta_norm_g, "delta_w_out": delta_w_out,
        "sconv_w_in": sconv_w_in, "sconv_w": sconv_w, "sconv_w_out": sconv_w_out,
        "swa_w_qkv": swa_w_qkv, "swa_sinks": swa_sinks, "swa_w_out": swa_w_out,
    }
    y_p, (ca_p, ds_p, dc_p, sc_p, k_p, v_p) = _run_group(x_prompt, (None,) * 6, w, True)
    y_s, (ca_s, ds_s, dc_s, sc_s, k_s, v_s) = _run_group(
        x_sample, (cache_conv_a, state_delta_s, state_delta_conv, cache_sconv, cache_swa_k, cache_swa_v), w, False)
    return (y_p, y_s, ca_p, ca_s, ds_p, ds_s, dc_p, dc_s, sc_p, sc_s, k_p, k_s, v_p, v_s)
```

```python
import functools
import math

import jax
import jax.numpy as jnp
from jax import lax
from jax.experimental import pallas as pl
from jax.experimental.pallas import tpu as pltpu

F32 = jnp.float32
BF16 = jnp.bfloat16
EPS = 1e-6
CHUNK = 64
WINDOW = 128
SWA_HEAD_DIM = 64
DN_HEAD_DIM = 128
REL_BUCKETS = 32
REL_MAX_DIST = 128
V7X_VMEM_BUDGET = 56 * 1024 * 1024
SUBLANES = 8
LANES = 128
NEG_INF = float("-inf")


def _cparams(sem):
    return pltpu.CompilerParams(dimension_semantics=sem, vmem_limit_bytes=V7X_VMEM_BUDGET)


def _rms(x, g):
    return x * lax.rsqrt(jnp.mean(x * x, axis=-1, keepdims=True) + EPS) * g


def _silu(x):
    return x * jax.nn.sigmoid(x)


def _mm(a, b):
    return jnp.dot(a.astype(BF16), b.astype(BF16), preferred_element_type=F32)


def _mm_nt(a, b):
    return lax.dot_general(a.astype(BF16), b.astype(BF16), (((1,), (1,)), ((), ())),
                           preferred_element_type=F32)


def _mm_tn(a, b):
    return lax.dot_general(a.astype(BF16), b.astype(BF16), (((0,), (0,)), ((), ())),
                           preferred_element_type=F32)


def _mm_f32(a, b):
    return jnp.dot(a, b, preferred_element_type=F32, precision=lax.Precision.HIGHEST)


def _pick_tile(n, pref):
    t = min(n, pref)
    while n % t:
        t //= 2
    return t


def _norm_mm_kernel(*refs, n_w, has_bias, combine, n_out):
    x_ref, g_ref = refs[0], refs[1]
    w_refs = refs[2:2 + n_w]
    pos = 2 + n_w
    b_refs = refs[pos:pos + n_w] if has_bias else ()
    pos += n_w if has_bias else 0
    o_refs = refs[pos:pos + n_out]
    h_ref = refs[pos + n_out]

    @pl.when(pl.program_id(1) == 0)
    def _():
        h_ref[...] = _rms(x_ref[...], g_ref[...]).astype(BF16)

    h = h_ref[...]
    ds = []
    for i in range(n_w):
        d = jnp.dot(h, w_refs[i][...], preferred_element_type=F32)
        if has_bias:
            d = d + b_refs[i][...]
        ds.append(d)
    outs = combine(*ds)
    for o_ref, o in zip(o_refs, outs):
        o_ref[...] = o.astype(o_ref.dtype)


def norm_matmul(x, g, ws, biases, combine, out_dtypes, tm=1024, tn=1024):
    t, k = x.shape
    n = ws[0].shape[1]
    tm = _pick_tile(t, tm)
    tn = _pick_tile(n, tn)
    n_w, n_out = len(ws), len(out_dtypes)
    has_bias = biases is not None
    in_specs = [pl.BlockSpec((tm, k), lambda i, j: (i, 0)),
                pl.BlockSpec((1, k), lambda i, j: (0, 0))]
    in_specs += [pl.BlockSpec((k, tn), lambda i, j: (0, j)) for _ in ws]
    args = [x, g.reshape(1, k)] + list(ws)
    if has_bias:
        in_specs += [pl.BlockSpec((1, tn), lambda i, j: (0, j)) for _ in ws]
        args += [b.reshape(1, n) for b in biases]
    outs = pl.pallas_call(
        functools.partial(_norm_mm_kernel, n_w=n_w, has_bias=has_bias, combine=combine, n_out=n_out),
        grid=(t // tm, n // tn),
        in_specs=in_specs,
        out_specs=[pl.BlockSpec((tm, tn), lambda i, j: (i, j)) for _ in out_dtypes],
        out_shape=[jax.ShapeDtypeStruct((t, n), dt) for dt in out_dtypes],
        scratch_shapes=[pltpu.VMEM((tm, k), BF16)],
        compiler_params=_cparams(("parallel", "arbitrary")),
        name="norm_matmul",
    )(*args)
    return outs


def _comb_id(d):
    return (d,)


def _comb_glu(a, b):
    return (a * jax.nn.sigmoid(b),)


def _comb_sconv(bg, cg, xin):
    return (bg, cg * xin)


def _mm_norm_res_kernel(a_ref, w_ref, g_ref, x_ref, o_ref):
    d = jnp.dot(a_ref[...], w_ref[...], preferred_element_type=F32)
    o_ref[...] = x_ref[...] + _rms(d, g_ref[...])


def matmul_norm_residual(a, w, g, x, tm=512):
    t, k = a.shape
    d = w.shape[1]
    tm = _pick_tile(t, tm)
    return pl.pallas_call(
        _mm_norm_res_kernel,
        grid=(t // tm,),
        in_specs=[pl.BlockSpec((tm, k), lambda i: (i, 0)),
                  pl.BlockSpec((k, d), lambda i: (0, 0), pipeline_mode=pl.Buffered(1)),
                  pl.BlockSpec((1, d), lambda i: (0, 0)),
                  pl.BlockSpec((tm, d), lambda i: (i, 0))],
        out_specs=pl.BlockSpec((tm, d), lambda i: (i, 0)),
        out_shape=jax.ShapeDtypeStruct((t, d), F32),
        compiler_params=_cparams(("parallel",)),
        name="matmul_norm_residual",
    )(a, w, g.reshape(1, d), x)


def _ffn_kernel(x_ref, g1_ref, wg_ref, wu_ref, wd_ref, g2_ref, o_ref, h_ref, *, nf, dchunk):
    f = pl.program_id(1)

    @pl.when(f == 0)
    def _():
        h_ref[...] = _rms(x_ref[...], g1_ref[...]).astype(BF16)
        o_ref[...] = jnp.zeros(o_ref.shape, F32)

    h = h_ref[...]
    gate = jnp.dot(h, wg_ref[...], preferred_element_type=F32)
    up = jnp.dot(h, wu_ref[...], preferred_element_type=F32)
    a = (_silu(gate) * up).astype(BF16)
    d = o_ref.shape[1]
    for c0 in range(0, d, dchunk):
        o_ref[:, c0:c0 + dchunk] += jnp.dot(a, wd_ref[:, c0:c0 + dchunk], preferred_element_type=F32)

    @pl.when(f == nf - 1)
    def _():
        o_ref[...] = x_ref[...] + _rms(o_ref[...], g2_ref[...])


def ffn(x, g1, wg, wu, wd, g2, tm=1024, tf=512, dchunk=512):
    t, d = x.shape
    fh = wg.shape[1]
    tm = _pick_tile(t, tm)
    tf = _pick_tile(fh, tf)
    nf = fh // tf
    dchunk = _pick_tile(d, dchunk)
    return pl.pallas_call(
        functools.partial(_ffn_kernel, nf=nf, dchunk=dchunk),
        grid=(t // tm, nf),
        in_specs=[pl.BlockSpec((tm, d), lambda i, f: (i, 0), pipeline_mode=pl.Buffered(1)),
                  pl.BlockSpec((1, d), lambda i, f: (0, 0)),
                  pl.BlockSpec((d, tf), lambda i, f: (0, f)),
                  pl.BlockSpec((d, tf), lambda i, f: (0, f)),
                  pl.BlockSpec((tf, d), lambda i, f: (f, 0)),
                  pl.BlockSpec((1, d), lambda i, f: (0, 0))],
        out_specs=pl.BlockSpec((tm, d), lambda i, f: (i, 0)),
        out_shape=jax.ShapeDtypeStruct((t, d), F32),
        scratch_shapes=[pltpu.VMEM((tm, d), BF16)],
        compiler_params=_cparams(("parallel", "arbitrary")),
        name="ffn",
    )(x, g1.reshape(1, d), wg, wu, wd, g2.reshape(1, d))


def _fill_ext(ext_ref, cur_ref, prev_ref, cache_ref, hb, tt):
    first = pl.program_id(1) == 0

    @pl.when(first)
    def _():
        ext_ref[0:hb, :] = cache_ref[0]

    @pl.when(jnp.logical_not(first))
    def _():
        ext_ref[0:hb, :] = prev_ref[0]

    ext_ref[hb:hb + tt, :] = cur_ref[0]


def _dwconv(ext_ref, w_ref, width, hb, tt, d, rows=64, cols=256):
    rows = min(rows, tt)
    cols = min(cols, d)
    base = hb - (width - 1)
    out_rows = []
    for r0 in range(0, tt, rows):
        out_cols = []
        for c0 in range(0, d, cols):
            acc = None
            for k in range(width):
                term = ext_ref[base + r0 + k:base + r0 + k + rows, c0:c0 + cols] * w_ref[k:k + 1, c0:c0 + cols]
                acc = term if acc is None else acc + term
            out_cols.append(acc)
        out_rows.append(jnp.concatenate(out_cols, axis=1))
    return jnp.concatenate(out_rows, axis=0)


def _conformer_tail_kernel(u_ref, prev_ref, cache_ref, dw_ref, dwb_ref, lng_ref, lnb_ref, w2_ref, b2_ref,
                           gpost_ref, x_ref, o_ref, ext_ref, *, width, hb, tt, d):
    _fill_ext(ext_ref, u_ref, prev_ref, cache_ref, hb, tt)
    c = _dwconv(ext_ref, dw_ref, width, hb, tt, d) + dwb_ref[...]
    cc = c - jnp.mean(c, axis=-1, keepdims=True)
    y = cc * lax.rsqrt(jnp.mean(cc * cc, axis=-1, keepdims=True) + EPS) * lng_ref[...] + lnb_ref[...]
    out = jnp.dot(_silu(y).astype(BF16), w2_ref[...], preferred_element_type=F32) + b2_ref[...]
    o_ref[0] = x_ref[0] + _rms(out, gpost_ref[...])


def _sconv_tail_kernel(gx_ref, prev_ref, cache_ref, bg_ref, cw_ref, w_ref, gpost_ref, x_ref, o_ref, ext_ref,
                       *, width, hb, tt, d):
    _fill_ext(ext_ref, gx_ref, prev_ref, cache_ref, hb, tt)
    c = _dwconv(ext_ref, cw_ref, width, hb, tt, d)
    out = jnp.dot((bg_ref[0] * c).astype(BF16), w_ref[...], preferred_element_type=F32)
    o_ref[0] = x_ref[0] + _rms(out, gpost_ref[...])


def _pad_rows(a, rows):
    pad = rows - a.shape[-2]
    cfg = [(0, 0)] * a.ndim
    cfg[-2] = (pad, 0)
    return jnp.pad(a, cfg)


def _halo_specs(tt, hb, d):
    step = tt // hb
    cur = pl.BlockSpec((1, tt, d), lambda b, i: (b, i, 0))
    prev = pl.BlockSpec((1, hb, d), lambda b, i: (b, jnp.maximum(i * step - 1, 0), 0))
    cache = pl.BlockSpec((1, hb, d), lambda b, i: (b, 0, 0))
    return cur, prev, cache


def _const_spec(shape, single=False):
    nd = len(shape)
    if single:
        return pl.BlockSpec(shape, lambda b, i: (0,) * nd, pipeline_mode=pl.Buffered(1))
    return pl.BlockSpec(shape, lambda b, i: (0,) * nd)


def conformer_tail(u, cache, dw, dwb, lng, lnb, w2, b2, gpost, x, tt=256):
    bsz, t, d = u.shape
    width = dw.shape[0]
    hb = 32
    tt = _pick_tile(t, tt)
    cur, prev, cch = _halo_specs(tt, hb, d)
    row = lambda a: a.reshape(1, d)
    return pl.pallas_call(
        functools.partial(_conformer_tail_kernel, width=width, hb=hb, tt=tt, d=d),
        grid=(bsz, t // tt),
        in_specs=[cur, prev, cch, _const_spec((hb, d)), _const_spec((1, d)), _const_spec((1, d)),
                  _const_spec((1, d)), _const_spec((d, d), single=True), _const_spec((1, d)),
                  _const_spec((1, d)), cur],
        out_specs=pl.BlockSpec((1, tt, d), lambda b, i: (b, i, 0)),
        out_shape=jax.ShapeDtypeStruct((bsz, t, d), F32),
        scratch_shapes=[pltpu.VMEM((hb + tt, d), F32)],
        compiler_params=_cparams(("parallel", "arbitrary")),
        name="conformer_tail",
    )(u, u, _pad_rows(cache, hb), jnp.pad(dw, ((0, hb - width), (0, 0))), row(dwb), row(lng), row(lnb),
      w2, row(b2), row(gpost), x)


def sconv_tail(gx, cache, bg, cw, w_out, gpost, x, tt=256):
    bsz, t, d = gx.shape
    width = cw.shape[0]
    hb = SUBLANES
    tt = _pick_tile(t, tt)
    cur, prev, cch = _halo_specs(tt, hb, d)
    return pl.pallas_call(
        functools.partial(_sconv_tail_kernel, width=width, hb=hb, tt=tt, d=d),
        grid=(bsz, t // tt),
        in_specs=[cur, prev, cch, cur, _const_spec((hb, d)), _const_spec((d, d), single=True),
                  _const_spec((1, d)), cur],
        out_specs=pl.BlockSpec((1, tt, d), lambda b, i: (b, i, 0)),
        out_shape=jax.ShapeDtypeStruct((bsz, t, d), F32),
        scratch_shapes=[pltpu.VMEM((hb + tt, d), F32)],
        compiler_params=_cparams(("parallel", "arbitrary")),
        name="sconv_tail",
    )(gx, gx, _pad_rows(cache, hb), bg, jnp.pad(cw, ((0, hb - width), (0, 0))), w_out, gpost.reshape(1, d), x)


def _tri_inverse(m, n):
    r = lax.broadcasted_iota(jnp.int32, (n, n), 0)
    c = lax.broadcasted_iota(jnp.int32, (n, n), 1)
    t = jnp.where(r == c, 1.0, 0.0).astype(F32) - m
    p = m
    for _ in range(int(math.log2(n)) - 1):
        p = _mm_f32(p, p)
        t = t + _mm_f32(t, p)
    return t


def _delta_kernel_v1(*refs, hp, L, has_state):
    nh = 2 * hp
    hd = DN_HEAD_DIM
    if has_state:
        (q_ref, k_ref, v_ref, z_ref, ba_ref, cq_ref, ck_ref, cv_ref, wq_ref, wk_ref, wv_ref, par_ref, ng_ref,
         s0_ref, o_ref, s_ref, eq_ref, ek_ref, ev_ref) = refs
    else:
        (q_ref, k_ref, v_ref, z_ref, ba_ref, cq_ref, ck_ref, cv_ref, wq_ref, wk_ref, wv_ref, par_ref, ng_ref,
         o_ref, s_ref, eq_ref, ek_ref, ev_ref) = refs
        s0_ref = None
    hb = SUBLANES
    width = 4

    @pl.when(pl.program_id(2) == 0)
    def _():
        eq_ref[0:hb, :] = cq_ref[0]
        ek_ref[0:hb, :] = ck_ref[0]
        ev_ref[0:hb, :] = cv_ref[0]
        if has_state:
            s_ref[...] = s0_ref[...]
        else:
            s_ref[...] = jnp.zeros(s_ref.shape, F32)

    def conv_silu(e_ref, cur_ref, w_ref):
        e_ref[hb:hb + L, :] = cur_ref[0]
        acc = None
        for kk in range(width):
            term = e_ref[hb - (width - 1) + kk:hb - (width - 1) + kk + L, :] * w_ref[kk:kk + 1, :]
            acc = term if acc is None else acc + term
        e_ref[0:hb, :] = e_ref[L:L + hb, :]
        return _silu(acc)

    q_all = conv_silu(eq_ref, q_ref, wq_ref)
    k_all = conv_silu(ek_ref, k_ref, wk_ref)
    v_all = conv_silu(ev_ref, v_ref, wv_ref)

    ba = ba_ref[0]
    beta_all = jax.nn.sigmoid(ba)
    g_all = -jnp.exp(par_ref[0, 0:1, :]) * jax.nn.softplus(ba + par_ref[0, 1:2, :])
    ri = lax.broadcasted_iota(jnp.int32, (L, L), 0)
    ci = lax.broadcasted_iota(jnp.int32, (L, L), 1)
    incl = ri >= ci
    strict = ri > ci
    gc_all = _mm_f32(jnp.where(incl, 1.0, 0.0).astype(F32), g_all)
    gc_t = gc_all.T

    for j in range(hp):
        qj = q_all[:, j * hd:(j + 1) * hd]
        kj = k_all[:, j * hd:(j + 1) * hd]
        qn = qj * lax.rsqrt(jnp.sum(qj * qj, axis=-1, keepdims=True) + EPS) * (hd ** -0.5)
        kn = kj * lax.rsqrt(jnp.sum(kj * kj, axis=-1, keepdims=True) + EPS)
        kk_raw = _mm_nt(kn, kn)
        qk_raw = _mm_nt(qn, kn)
        for r in range(2):
            h = 2 * j + r
            vh = v_all[:, h * hd:(h + 1) * hd]
            gcol = gc_all[:, 64 + h:65 + h]
            grow = gc_t[64 + h:65 + h, :]
            bcol = beta_all[:, h:h + 1]
            glast = gc_all[L - 1:L, 64 + h:65 + h]
            decay = jnp.exp(jnp.where(incl, gcol - grow, NEG_INF))
            m = jnp.where(strict, bcol * kk_raw * decay, 0.0)
            tinv = _tri_inverse(m, L)
            egc = jnp.exp(gcol)
            rhs = jnp.concatenate([bcol * vh, (bcol * egc) * kn], axis=-1)
            sol = _mm(tinv, rhs)
            u, w = sol[:, :hd], sol[:, hd:]
            qkm = jnp.where(incl, qk_raw * decay, 0.0)
            qg = qn * egc
            kg = kn * jnp.exp(glast - gcol)
            s = s_ref[0, h]
            v_new = u - _mm(w, s)
            o = _mm(qg, s) + _mm(qkm, v_new)
            s_ref[0, h] = s * jnp.exp(glast) + _mm_tn(kg, v_new)
            zh = z_ref[0, :, h * hd:(h + 1) * hd]
            o = o * lax.rsqrt(jnp.mean(o * o, axis=-1, keepdims=True) + EPS) * ng_ref[...] * _silu(zh)
            o_ref[0, :, h * hd:(h + 1) * hd] = o.astype(o_ref.dtype)


def _delta_kernel(*refs, hp, L, has_state):
    hd = DN_HEAD_DIM
    if has_state:
        (q_ref, k_ref, v_ref, z_ref, ba_ref, cq_ref, ck_ref, cv_ref, wq_ref, wk_ref, wv_ref, par_ref, ng_ref,
         s0_ref, o_ref, s_ref, eq_ref, ek_ref, ev_ref) = refs
    else:
        (q_ref, k_ref, v_ref, z_ref, ba_ref, cq_ref, ck_ref, cv_ref, wq_ref, wk_ref, wv_ref, par_ref, ng_ref,
         o_ref, s_ref, eq_ref, ek_ref, ev_ref) = refs
        s0_ref = None
    hb = SUBLANES
    width = 4
    L2 = 2 * L

    @pl.when(pl.program_id(2) == 0)
    def _():
        eq_ref[0:hb, :] = cq_ref[0]
        ek_ref[0:hb, :] = ck_ref[0]
        ev_ref[0:hb, :] = cv_ref[0]
        if has_state:
            s_ref[...] = s0_ref[...]
        else:
            s_ref[...] = jnp.zeros(s_ref.shape, F32)

    eq_ref[hb:hb + L, :] = q_ref[0]
    ek_ref[hb:hb + L, :] = k_ref[0]
    ev_ref[hb:hb + L, :] = v_ref[0]

    def conv_silu(e_ref, w_ref, lo):
        acc = None
        for kk in range(width):
            r0 = hb - (width - 1) + kk
            term = e_ref[r0:r0 + L, lo:lo + hd] * w_ref[kk:kk + 1, lo:lo + hd]
            acc = term if acc is None else acc + term
        return _silu(acc)

    ba = ba_ref[0]
    beta_all = jax.nn.sigmoid(ba)
    g_all = -jnp.exp(par_ref[0, 0:1, :]) * jax.nn.softplus(ba + par_ref[0, 1:2, :])
    ri = lax.broadcasted_iota(jnp.int32, (L, L), 0)
    ci = lax.broadcasted_iota(jnp.int32, (L, L), 1)
    gc_all = _mm_f32(jnp.where(ri >= ci, 1.0, 0.0).astype(F32), g_all)
    gc_t = gc_all.T
    grow_all = jnp.concatenate([gc_t[64:64 + hp], gc_t[96:96 + hp]], axis=1)
    lane_h = lax.broadcasted_iota(jnp.int32, (hp, L2), 1)
    gl_e = jnp.broadcast_to(grow_all[:, L - 1:L], (hp, L2))
    gl_o = jnp.broadcast_to(grow_all[:, L2 - 1:L2], (hp, L2))
    kdec_all = jnp.exp(jnp.where(lane_h < L, gl_e, gl_o) - grow_all)
    egl_e = jnp.exp(gl_e)
    egl_o = jnp.exp(gl_o)

    r2 = lax.broadcasted_iota(jnp.int32, (L, L2), 0)
    l2 = lax.broadcasted_iota(jnp.int32, (L, L2), 1)
    c2 = jnp.bitwise_and(l2, L - 1)
    incl2 = r2 >= c2
    strict2 = r2 > c2
    first2 = l2 < L
    eye2 = jnp.where(r2 == c2, 1.0, 0.0).astype(F32)
    rb = lax.broadcasted_iota(jnp.int32, (L2, L2), 0)
    lb = lax.broadcasted_iota(jnp.int32, (L2, L2), 1)
    bmask = (rb < L) == (lb < L)
    zero_sq = jnp.zeros((L, hd), F32)

    def bdiag(x):
        xb = x.astype(BF16)
        return jnp.where(bmask, jnp.concatenate([xb, xb], axis=0), jnp.zeros((), BF16))

    def bcast(col):
        return jnp.broadcast_to(col, (L, hd))

    pairs = range(hp)
    qn, kn, knt2 = [], [], []
    for j in pairs:
        qj = conv_silu(eq_ref, wq_ref, j * hd)
        kj = conv_silu(ek_ref, wk_ref, j * hd)
        qn.append(qj * lax.rsqrt(jnp.sum(qj * qj, axis=-1, keepdims=True) + EPS) * (hd ** -0.5))
        kn.append(kj * lax.rsqrt(jnp.sum(kj * kj, axis=-1, keepdims=True) + EPS))
        knt2.append(jnp.concatenate([kn[j], kn[j]], axis=0).T)

    a2 = [_mm(jnp.concatenate([qn[j], kn[j]], axis=0), knt2[j]) for j in pairs]

    gcs, bcs, egcs, tmat, pmat, qkm2 = [], [], [], [], [], []
    for j in pairs:
        gce, gco = bcast(gc_all[:, 64 + j:65 + j]), bcast(gc_all[:, 96 + j:97 + j])
        bce, bco = bcast(beta_all[:, j:j + 1]), bcast(beta_all[:, 32 + j:33 + j])
        gcs.append((gce, gco))
        bcs.append((bce, bco))
        egcs.append((jnp.exp(gce), jnp.exp(gco)))
        gcol2 = jnp.where(first2, gce, gco)
        bcol2 = jnp.where(first2, bce, bco)
        decay2 = jnp.exp(jnp.where(incl2, gcol2 - grow_all[j:j + 1, :], NEG_INF))
        m2 = jnp.where(strict2, bcol2 * a2[j][L:] * decay2, 0.0)
        qkm2.append(jnp.where(incl2, a2[j][:L] * decay2, 0.0))
        tmat.append(eye2 - m2)
        pmat.append(m2)

    nst = int(math.log2(L)) - 1
    pmat = [_mm(pmat[j], bdiag(pmat[j])) for j in pairs]
    for st in range(nst):
        if st < nst - 1:
            outs = [_mm(jnp.concatenate([tmat[j], pmat[j]], axis=0), bdiag(pmat[j])) for j in pairs]
            tmat = [tmat[j] + outs[j][:L] for j in pairs]
            pmat = [outs[j][L:] for j in pairs]
        else:
            tmat = [tmat[j] + _mm(tmat[j], bdiag(pmat[j])) for j in pairs]

    heads = [(j, r) for j in pairs for r in range(2)]
    sols = []
    for j, r in heads:
        h = 2 * j + r
        vh = conv_silu(ev_ref, wv_ref, h * hd)
        bc_, egc_ = bcs[j][r], egcs[j][r]
        rhs = jnp.concatenate([bc_ * vh, (bc_ * egc_) * kn[j]], axis=1).astype(BF16)
        zr = jnp.zeros_like(rhs)
        rhs_pad = jnp.concatenate([rhs, zr] if r == 0 else [zr, rhs], axis=0)
        sols.append(_mm(tmat[j], rhs_pad))

    xs = []
    for idx, (j, r) in enumerate(heads):
        h = 2 * j + r
        w_ = sols[idx][:, hd:]
        xs.append(_mm(jnp.concatenate([w_, qn[j] * egcs[j][r]], axis=0), s_ref[0, h]))

    kgt2 = [knt2[j] * kdec_all[j:j + 1, :] for j in pairs]
    for idx, (j, r) in enumerate(heads):
        h = 2 * j + r
        v_new = sols[idx][:, :hd] - xs[idx][:L]
        vpad = jnp.concatenate([v_new, zero_sq] if r == 0 else [zero_sq, v_new], axis=0)
        y = _mm(jnp.concatenate([qkm2[j], kgt2[j]], axis=0), vpad)
        egl = (egl_e if r == 0 else egl_o)[j:j + 1, :]
        s_ref[0, h] = s_ref[0, h] * egl + y[L:]
        o = xs[idx][L:] + y[:L]
        zh = z_ref[0, :, h * hd:(h + 1) * hd]
        o = o * lax.rsqrt(jnp.mean(o * o, axis=-1, keepdims=True) + EPS) * ng_ref[...] * _silu(zh)
        o_ref[0, :, h * hd:(h + 1) * hd] = o.astype(o_ref.dtype)

    eq_ref[0:hb, :] = eq_ref[L:L + hb, :]
    ek_ref[0:hb, :] = ek_ref[L:L + hb, :]
    ev_ref[0:hb, :] = ev_ref[L:L + hb, :]


def delta_rule(qkv, z, ba, conv_cache, conv_w, par, norm_g, s0, hp):
    bsz, t, _ = qkv.shape
    vdim = z.shape[-1]
    hd = DN_HEAD_DIM
    vh = vdim // hd
    qh = vh // 2
    hg = qh // hp
    nh = 2 * hp
    L = CHUNK
    nc = t // L
    hb = SUBLANES
    cache = _pad_rows(conv_cache, hb)
    cw = jnp.pad(conv_w, ((0, hb - conv_w.shape[0]), (0, 0)))
    qw, vw = hp * hd, nh * hd
    koff, voff = qh // hp, 2 * qh // nh
    in_specs = [
        pl.BlockSpec((1, L, qw), lambda b, g, c: (b, c, g)),
        pl.BlockSpec((1, L, qw), lambda b, g, c: (b, c, koff + g)),
        pl.BlockSpec((1, L, vw), lambda b, g, c: (b, c, voff + g)),
        pl.BlockSpec((1, L, vw), lambda b, g, c: (b, c, g)),
        pl.BlockSpec((1, L, LANES), lambda b, g, c: (b, c, g)),
        pl.BlockSpec((1, hb, qw), lambda b, g, c: (b, 0, g)),
        pl.BlockSpec((1, hb, qw), lambda b, g, c: (b, 0, koff + g)),
        pl.BlockSpec((1, hb, vw), lambda b, g, c: (b, 0, voff + g)),
        pl.BlockSpec((hb, qw), lambda b, g, c: (0, g)),
        pl.BlockSpec((hb, qw), lambda b, g, c: (0, koff + g)),
        pl.BlockSpec((hb, vw), lambda b, g, c: (0, voff + g)),
        pl.BlockSpec((1, hb, LANES), lambda b, g, c: (g, 0, 0)),
        pl.BlockSpec((1, hd), lambda b, g, c: (0, 0)),
    ]
    args = [qkv, qkv, qkv, z, ba, cache, cache, cache, cw, cw, cw, par, norm_g.reshape(1, hd)]
    if s0 is not None:
        in_specs.append(pl.BlockSpec((1, nh, hd, hd), lambda b, g, c: (b, g, 0, 0)))
        args.append(s0)
    o, s = pl.pallas_call(
        functools.partial(_delta_kernel, hp=hp, L=L, has_state=s0 is not None),
        grid=(bsz, hg, nc),
        in_specs=in_specs,
        out_specs=[pl.BlockSpec((1, L, vw), lambda b, g, c: (b, c, g)),
                   pl.BlockSpec((1, nh, hd, hd), lambda b, g, c: (b, g, 0, 0))],
        out_shape=[jax.ShapeDtypeStruct((bsz, t, vdim), BF16),
                   jax.ShapeDtypeStruct((bsz, vh, hd, hd), F32)],
        scratch_shapes=[pltpu.VMEM((hb + L, qw), F32), pltpu.VMEM((hb + L, qw), F32),
                        pltpu.VMEM((hb + L, vw), F32)],
        compiler_params=_cparams(("parallel", "parallel", "arbitrary")),
        name="delta_rule",
    )(*args)
    return o, s


def _attn_kernel(q_ref, k0_ref, k1_ref, k2_ref, v0_ref, v1_ref, v2_ref, bias_ref, sink_ref, o_ref,
                 *, nkv, grp, masked):
    hd = SWA_HEAD_DIM
    c = pl.program_id(1)
    q = q_ref[0]
    k = jnp.concatenate([k0_ref[0], k1_ref[0], k2_ref[0]], axis=0)
    v = jnp.concatenate([v0_ref[0], v1_ref[0], v2_ref[0]], axis=0)
    nq, nk = q.shape[0], k.shape[0]
    if masked:
        kpos = c * CHUNK - WINDOW + lax.broadcasted_iota(jnp.int32, (grp * nq, nk), 1)
        valid = kpos >= 0
    heads = range(nkv)
    qs = [jnp.concatenate([q[:, (n * grp + g) * hd:(n * grp + g + 1) * hd] for g in range(grp)], axis=0)
          for n in heads]
    ss = [_mm_nt(qs[n], k[:, n * hd:(n + 1) * hd]) * (hd ** -0.5) + bias_ref[n] for n in heads]
    if masked:
        ss = [jnp.where(valid, s, NEG_INF) for s in ss]
    mxs = [jnp.maximum(jnp.max(ss[n], axis=-1, keepdims=True), sink_ref[n]) for n in heads]
    ps = [jnp.exp(ss[n] - mxs[n]) for n in heads]
    dens = [jnp.sum(ps[n], axis=-1, keepdims=True) + jnp.exp(sink_ref[n] - mxs[n]) for n in heads]
    os_ = [_mm(ps[n], v[:, n * hd:(n + 1) * hd]) / dens[n] for n in heads]
    for n in heads:
        for g in range(grp):
            o_ref[0, :, (n * grp + g) * hd:(n * grp + g + 1) * hd] = (
                os_[n][g * nq:(g + 1) * nq].astype(o_ref.dtype))


def _t5_bucket(rel):
    half = REL_BUCKETS // 2
    max_exact = half // 2
    a = jnp.abs(rel)
    af = jnp.maximum(a, 1).astype(F32)
    large = max_exact + (jnp.log(af / max_exact) / math.log(REL_MAX_DIST / max_exact)
                         * (half - max_exact)).astype(jnp.int32)
    large = jnp.minimum(large, half - 1)
    return jnp.where(rel > 0, half, 0) + jnp.where(a < max_exact, a, large)


def _bias_table(rel_bias, nkv, grp, n_q, n_k):
    rel = jnp.arange(n_k)[None, :] - WINDOW - jnp.arange(n_q)[:, None]
    bias = jnp.take(rel_bias, _t5_bucket(rel), axis=0).astype(F32)
    return jnp.transpose(bias, (2, 0, 1)).reshape(nkv, grp * n_q, n_k)


def swa_attention(q_src, k_src, v_src, kv_col, rel_bias, sinks, nkv, masked):
    hd = SWA_HEAD_DIM
    nheads = sinks.shape[0]
    grp = nheads // nkv
    qd, kvd = nheads * hd, nkv * hd
    bsz = q_src.shape[0]
    t = k_src.shape[1] if masked else k_src.shape[1] - WINDOW
    nc = t // CHUNK
    bias = _bias_table(rel_bias, nkv, grp, CHUNK, WINDOW + CHUNK)
    sink = jnp.repeat(sinks.astype(F32).reshape(nkv, grp), CHUNK, axis=1).reshape(nkv, grp * CHUNK, 1)
    if masked:
        rows = [lambda b, c, j=j: jnp.maximum(c + j - 2, 0) for j in range(3)]
    else:
        rows = [lambda b, c, j=j: c + j for j in range(3)]
    kspecs = [pl.BlockSpec((1, CHUNK, kvd), lambda b, c, r=r: (b, r(b, c), kv_col[0])) for r in rows]
    vspecs = [pl.BlockSpec((1, CHUNK, kvd), lambda b, c, r=r: (b, r(b, c), kv_col[1])) for r in rows]
    return pl.pallas_call(
        functools.partial(_attn_kernel, nkv=nkv, grp=grp, masked=masked),
        grid=(bsz, nc),
        in_specs=[pl.BlockSpec((1, CHUNK, qd), lambda b, c: (b, c, 0))] + kspecs + vspecs
        + [pl.BlockSpec((nkv, grp * CHUNK, WINDOW + CHUNK), lambda b, c: (0, 0, 0)),
           pl.BlockSpec((nkv, grp * CHUNK, 1), lambda b, c: (0, 0, 0))],
        out_specs=pl.BlockSpec((1, CHUNK, qd), lambda b, c: (b, c, 0)),
        out_shape=jax.ShapeDtypeStruct((bsz, t, qd), BF16),
        compiler_params=_cparams(("parallel", "arbitrary")),
        name="swa_attention",
    )(q_src, k_src, k_src, k_src, v_src, v_src, v_src, bias, sink)


def _delta_heads_per_step(qh):
    return min(16, qh)


def _run_group(x, caches, w, first_chunk):
    conv_a, delta_s, delta_conv, sconv, swa_k, swa_v = caches
    bsz, t, d = x.shape
    depth = w["norm_mix_pre"].shape[0]
    xf = x.reshape(bsz * t, d)
    new = {k: [] for k in ("conv_a", "ds", "dc", "sc", "k", "v")}
    bf = lambda a: a.astype(BF16)
    for i in range(depth):
        mix, j = i % 4, i // 4
        g_pre, g_post = w["norm_mix_pre"][i], w["norm_mix_post"][i]
        if mix == 0:
            w1 = bf(w["conv_a_w1"][j])
            b1 = w["conv_a_b1"][j]
            (u,) = norm_matmul(xf, g_pre, [w1[:, :d], w1[:, d:]], [b1[:d], b1[d:]], _comb_glu, [F32], tm=512)
            u = u.reshape(bsz, t, d)
            width = w["conv_a_dw"].shape[1]
            cache = conv_a[j] if conv_a is not None else jnp.zeros((bsz, width - 1, d), F32)
            xf = conformer_tail(u, cache, w["conv_a_dw"][j], w["conv_a_dw_b"][j], w["conv_a_ln_g"][j],
                                w["conv_a_ln_b"][j], bf(w["conv_a_w2"][j]), w["conv_a_b2"][j], g_post,
                                xf.reshape(bsz, t, d)).reshape(bsz * t, d)
            new["conv_a"].append(jnp.concatenate([cache, u], axis=1)[:, -(width - 1):] if t < width - 1
                                 else u[:, -(width - 1):])
        elif mix == 1:
            w_in = w["delta_w_in"][j]
            vh = w["delta_a_log"].shape[1]
            vdim = vh * DN_HEAD_DIM
            qkvd = w["delta_conv_w"].shape[2]
            qh = (qkvd - vdim) // (2 * DN_HEAD_DIM)
            hp = _delta_heads_per_step(qh)
            nh, hg = 2 * hp, qh // hp
            (qkv,) = norm_matmul(xf, g_pre, [bf(w_in[:, :qkvd])], None, _comb_id, [F32], tn=1024)
            (z,) = norm_matmul(xf, g_pre, [bf(w_in[:, qkvd:qkvd + vdim])], None, _comb_id, [F32], tn=1024)
            def lanes4(be, bo, ae, ao):
                pad = [(0, 0)] * (be.ndim - 1) + [(0, 32 - hp)]
                return jnp.concatenate([jnp.pad(a, pad) for a in (be, bo, ae, ao)], axis=-1)

            wb = w_in[:, qkvd + vdim:qkvd + vdim + vh].reshape(d, hg, hp, 2)
            wa = w_in[:, qkvd + vdim + vh:].reshape(d, hg, hp, 2)
            w_ba = lanes4(wb[..., 0], wb[..., 1], wa[..., 0], wa[..., 1]).reshape(d, hg * LANES)
            (ba,) = norm_matmul(xf, g_pre, [bf(w_ba)], None, _comb_id, [F32], tn=LANES)
            zl = jnp.zeros((hg, hp), F32)
            alog = w["delta_a_log"][j].reshape(hg, hp, 2)
            dtb = w["delta_dt_bias"][j].reshape(hg, hp, 2)
            par = jnp.stack([lanes4(zl, zl, alog[..., 0], alog[..., 1]),
                             lanes4(zl, zl, dtb[..., 0], dtb[..., 1])], axis=1)
            par = jnp.pad(par, ((0, 0), (0, SUBLANES - 2), (0, 0)))
            cw = w["delta_conv_w"][j]
            cache = delta_conv[j] if delta_conv is not None else jnp.zeros((bsz, cw.shape[0] - 1, qkvd), F32)
            qkv3 = qkv.reshape(bsz, t, qkvd)
            o, s_new = delta_rule(qkv3, z.reshape(bsz, t, vdim), ba.reshape(bsz, t, hg * LANES), cache, cw, par,
                                  w["delta_norm_g"][j], delta_s[j] if delta_s is not None else None, hp)
            xf = matmul_norm_residual(o.reshape(bsz * t, vdim), bf(w["delta_w_out"][j]), g_post, xf)
            new["ds"].append(s_new)
            new["dc"].append(qkv3[:, -(cw.shape[0] - 1):])
        elif mix == 2:
            w_in = bf(w["sconv_w_in"][j])
            bg, gx = norm_matmul(xf, g_pre, [w_in[:, :d], w_in[:, d:2 * d], w_in[:, 2 * d:]], None, _comb_sconv,
                                 [F32, F32], tm=512, tn=512)
            gx = gx.reshape(bsz, t, d)
            cw = w["sconv_w"][j]
            cache = sconv[j] if sconv is not None else jnp.zeros((bsz, cw.shape[0] - 1, d), F32)
            xf = sconv_tail(gx, cache, bg.reshape(bsz, t, d), cw, bf(w["sconv_w_out"][j]), g_post,
                            xf.reshape(bsz, t, d)).reshape(bsz * t, d)
            new["sc"].append(gx[:, -(cw.shape[0] - 1):])
        else:
            w_qkv = w["swa_w_qkv"][j]
            nheads = w["swa_sinks"].shape[1]
            qd = nheads * SWA_HEAD_DIM
            kvd = (w_qkv.shape[1] - qd) // 2
            nkv = kvd // SWA_HEAD_DIM
            (qkv,) = norm_matmul(xf, g_pre, [bf(w_qkv)], None, _comb_id, [F32])
            qkv = qkv.reshape(bsz, t, qd + 2 * kvd)
            if first_chunk:
                o = swa_attention(qkv, qkv, qkv, (qd // kvd, qd // kvd + 1), w["rel_bias"], w["swa_sinks"][j],
                                  nkv, True)
                k_ext, v_ext = qkv[:, :, qd:qd + kvd], qkv[:, :, qd + kvd:]
            else:
                k_ext = jnp.concatenate([swa_k[j].reshape(bsz, WINDOW, kvd), qkv[:, :, qd:qd + kvd]], axis=1)
                v_ext = jnp.concatenate([swa_v[j].reshape(bsz, WINDOW, kvd), qkv[:, :, qd + kvd:]], axis=1)
                o = swa_attention(qkv, k_ext, v_ext, (0, 0), w["rel_bias"], w["swa_sinks"][j], nkv, False)
            xf = matmul_norm_residual(o.reshape(bsz * t, qd), bf(w["swa_w_out"][j]), g_post, xf)
            new["k"].append(k_ext[:, -WINDOW:].reshape(bsz, WINDOW, nkv, SWA_HEAD_DIM))
            new["v"].append(v_ext[:, -WINDOW:].reshape(bsz, WINDOW, nkv, SWA_HEAD_DIM))
        xf = ffn(xf, w["norm_ffn_pre"][i], bf(w["ffn_w_gate"][i]), bf(w["ffn_w_up"][i]), bf(w["ffn_w_down"][i]),
                 w["norm_ffn_post"][i])
    return xf.reshape(bsz, t, d), tuple(jnp.stack(new[k]) for k in ("conv_a", "ds", "dc", "sc", "k", "v"))


def kernel(x_prompt, x_sample, cache_conv_a, state_delta_s, state_delta_conv, cache_sconv, cache_swa_k, cache_swa_v, rel_bias, norm_mix_pre, norm_mix_post, norm_ffn_pre, norm_ffn_post, ffn_w_gate, ffn_w_up, ffn_w_down, conv_a_w1, conv_a_b1, conv_a_dw, conv_a_dw_b, conv_a_ln_g, conv_a_ln_b, conv_a_w2, conv_a_b2, delta_w_in, delta_conv_w, delta_a_log, delta_dt_bias, delta_norm_g, delta_w_out, sconv_w_in, sconv_w, sconv_w_out, swa_w_qkv, swa_sinks, swa_w_out):
    w = {
        "rel_bias": rel_bias, "norm_mix_pre": norm_mix_pre, "norm_mix_post": norm_mix_post,
        "norm_ffn_pre": norm_ffn_pre, "norm_ffn_post": norm_ffn_post, "ffn_w_gate": ffn_w_gate,
        "ffn_w_up": ffn_w_up, "ffn_w_down": ffn_w_down, "conv_a_w1": conv_a_w1, "conv_a_b1": conv_a_b1,
        "conv_a_dw": conv_a_dw, "conv_a_dw_b": conv_a_dw_b, "conv_a_ln_g": conv_a_ln_g,
        "conv_a_ln_b": conv_a_ln_b, "conv_a_w2": conv_a_w2, "conv_a_b2": conv_a_b2,
        "delta_w_in": delta_w_in, "delta_conv_w": delta_conv_w, "delta_a_log": delta_a_log,
        "delta_dt_bias": delta_dt_bias, "delta_norm_g": delta_norm_g, "delta_w_out": delta_w_out,
        "sconv_w_in": sconv_w_in, "sconv_w": sconv_w, "sconv_w_out": sconv_w_out,
        "swa_w_qkv": swa_w_qkv, "swa_sinks": swa_sinks, "swa_w_out": swa_w_out,
    }
    y_p, (ca_p, ds_p, dc_p, sc_p, k_p, v_p) = _run_group(x_prompt, (None,) * 6, w, True)
    y_s, (ca_s, ds_s, dc_s, sc_s, k_s, v_s) = _run_group(
        x_sample, (cache_conv_a, state_delta_s, state_delta_conv, cache_sconv, cache_swa_k, cache_swa_v), w, False)
    return (y_p, y_s, ca_p, ca_s, ds_p, ds_s, dc_p, dc_s, sc_p, sc_s, k_p, k_s, v_p, v_s)
```

```python
import functools
import math

import jax
import jax.numpy as jnp
from jax import lax
from jax.experimental import pallas as pl
from jax.experimental.pallas import tpu as pltpu

F32 = jnp.float32
BF16 = jnp.bfloat16
EPS = 1e-6
CHUNK = 64
WINDOW = 128
SWA_HEAD_DIM = 64
DN_HEAD_DIM = 128
REL_BUCKETS = 32
REL_MAX_DIST = 128
V7X_VMEM_BUDGET = 56 * 1024 * 1024
SUBLANES = 8
LANES = 128
NEG_INF = float("-inf")


def _cparams(sem, vmem=V7X_VMEM_BUDGET):
    return pltpu.CompilerParams(dimension_semantics=sem, vmem_limit_bytes=vmem)


def _rms(x, g):
    return x * lax.rsqrt(jnp.mean(x * x, axis=-1, keepdims=True) + EPS) * g


def _silu(x):
    return x * jax.nn.sigmoid(x)


def _mm(a, b):
    return jnp.dot(a.astype(BF16), b.astype(BF16), preferred_element_type=F32)


def _mm_nt(a, b):
    return lax.dot_general(a.astype(BF16), b.astype(BF16), (((1,), (1,)), ((), ())),
                           preferred_element_type=F32)


def _mm_f32(a, b):
    return jnp.dot(a, b, preferred_element_type=F32, precision=lax.Precision.HIGHEST)


def _pick_tile(n, pref):
    t = min(n, pref)
    while n % t:
        t //= 2
    return t


def _pad_rows(a, rows):
    pad = rows - a.shape[-2]
    cfg = [(0, 0)] * a.ndim
    cfg[-2] = (pad, 0)
    return jnp.pad(a, cfg)


def _norm_mm_kernel(*refs, n_w, has_bias, combine, n_out):
    x_ref, g_ref = refs[0], refs[1]
    w_refs = refs[2:2 + n_w]
    pos = 2 + n_w
    b_refs = refs[pos:pos + n_w] if has_bias else ()
    pos += n_w if has_bias else 0
    o_refs = refs[pos:pos + n_out]
    h_ref = refs[pos + n_out]

    @pl.when(pl.program_id(1) == 0)
    def _():
        h_ref[...] = _rms(x_ref[...], g_ref[...]).astype(BF16)

    h = h_ref[...]
    ds = []
    for i in range(n_w):
        d = jnp.dot(h, w_refs[i][...], preferred_element_type=F32)
        if has_bias:
            d = d + b_refs[i][...]
        ds.append(d)
    outs = combine(*ds)
    for o_ref, o in zip(o_refs, outs):
        o_ref[...] = o.astype(o_ref.dtype)


def norm_matmul(x, g, ws, biases, combine, out_dtypes, tm=1024, tn=1024):
    t, k = x.shape
    n = ws[0].shape[1]
    tm = _pick_tile(t, tm)
    tn = _pick_tile(n, tn)
    n_w, n_out = len(ws), len(out_dtypes)
    has_bias = biases is not None
    in_specs = [pl.BlockSpec((tm, k), lambda i, j: (i, 0)),
                pl.BlockSpec((1, k), lambda i, j: (0, 0))]
    in_specs += [pl.BlockSpec((k, tn), lambda i, j: (0, j)) for _ in ws]
    args = [x, g.reshape(1, k)] + list(ws)
    if has_bias:
        in_specs += [pl.BlockSpec((1, tn), lambda i, j: (0, j)) for _ in ws]
        args += [b.reshape(1, n) for b in biases]
    outs = pl.pallas_call(
        functools.partial(_norm_mm_kernel, n_w=n_w, has_bias=has_bias, combine=combine, n_out=n_out),
        grid=(t // tm, n // tn),
        in_specs=in_specs,
        out_specs=[pl.BlockSpec((tm, tn), lambda i, j: (i, j)) for _ in out_dtypes],
        out_shape=[jax.ShapeDtypeStruct((t, n), dt) for dt in out_dtypes],
        scratch_shapes=[pltpu.VMEM((tm, k), BF16)],
        compiler_params=_cparams(("parallel", "arbitrary")),
        name="norm_matmul",
    )(*args)
    return outs


def _comb_id(d):
    return (d,)


def _comb_sconv(bg, cg, xin):
    return (bg, cg * xin)


def _mm_norm_res_kernel(a_ref, w_ref, g_ref, x_ref, o_ref):
    d = jnp.dot(a_ref[...], w_ref[...], preferred_element_type=F32)
    o_ref[...] = x_ref[...] + _rms(d, g_ref[...])


def matmul_norm_residual(a, w, g, x, tm=512):
    t, k = a.shape
    d = w.shape[1]
    tm = _pick_tile(t, tm)
    return pl.pallas_call(
        _mm_norm_res_kernel,
        grid=(t // tm,),
        in_specs=[pl.BlockSpec((tm, k), lambda i: (i, 0)),
                  pl.BlockSpec((k, d), lambda i: (0, 0), pipeline_mode=pl.Buffered(1)),
                  pl.BlockSpec((1, d), lambda i: (0, 0)),
                  pl.BlockSpec((tm, d), lambda i: (i, 0))],
        out_specs=pl.BlockSpec((tm, d), lambda i: (i, 0)),
        out_shape=jax.ShapeDtypeStruct((t, d), F32),
        compiler_params=_cparams(("parallel",)),
        name="matmul_norm_residual",
    )(a, w, g.reshape(1, d), x)


def _ffn_kernel(x_ref, g1_ref, wg_ref, wu_ref, wd_ref, g2_ref, o_ref, h_ref, *, nf, dchunk):
    f = pl.program_id(1)

    @pl.when(f == 0)
    def _():
        h_ref[...] = _rms(x_ref[...], g1_ref[...]).astype(BF16)
        o_ref[...] = jnp.zeros(o_ref.shape, F32)

    h = h_ref[...]
    gate = jnp.dot(h, wg_ref[...], preferred_element_type=F32)
    up = jnp.dot(h, wu_ref[...], preferred_element_type=F32)
    a = (_silu(gate) * up).astype(BF16)
    d = o_ref.shape[1]
    for c0 in range(0, d, dchunk):
        o_ref[:, c0:c0 + dchunk] += jnp.dot(a, wd_ref[:, c0:c0 + dchunk], preferred_element_type=F32)

    @pl.when(f == nf - 1)
    def _():
        o_ref[...] = x_ref[...] + _rms(o_ref[...], g2_ref[...])


def ffn(x, g1, wg, wu, wd, g2, tm=1024, tf=512, dchunk=512):
    t, d = x.shape
    fh = wg.shape[1]
    tm = _pick_tile(t, tm)
    tf = _pick_tile(fh, tf)
    nf = fh // tf
    dchunk = _pick_tile(d, dchunk)
    return pl.pallas_call(
        functools.partial(_ffn_kernel, nf=nf, dchunk=dchunk),
        grid=(t // tm, nf),
        in_specs=[pl.BlockSpec((tm, d), lambda i, f: (i, 0), pipeline_mode=pl.Buffered(1)),
                  pl.BlockSpec((1, d), lambda i, f: (0, 0)),
                  pl.BlockSpec((d, tf), lambda i, f: (0, f)),
                  pl.BlockSpec((d, tf), lambda i, f: (0, f)),
                  pl.BlockSpec((tf, d), lambda i, f: (f, 0)),
                  pl.BlockSpec((1, d), lambda i, f: (0, 0))],
        out_specs=pl.BlockSpec((tm, d), lambda i, f: (i, 0)),
        out_shape=jax.ShapeDtypeStruct((t, d), F32),
        scratch_shapes=[pltpu.VMEM((tm, d), BF16)],
        compiler_params=_cparams(("parallel", "arbitrary")),
        name="ffn",
    )(x, g1.reshape(1, d), wg, wu, wd, g2.reshape(1, d))


def _dwconv(ext_ref, zs_ref, w_ref, out_ref, width, hb, tt, d, cb):
    for c0 in range(0, d, cb):
        _dwconv_block(ext_ref, zs_ref, w_ref, None, out_ref, width, hb, tt, c0, cb)


ROW_CHUNK = 64
COL_CHUNK = 256


def _dwconv_block(ext_ref, zs_ref, w_ref, bias_ref, out_ref, width, hb, tt, c0, cb):
    base = hb - (width - 1)
    taps = {}
    for k in range(width):
        taps.setdefault((base + k) % SUBLANES, []).append(k)
    rows = tt + SUBLANES
    cch = min(COL_CHUNK, cb)
    for cc in range(0, cb, cch):
        for r in range(0, rows, ROW_CHUNK):
            nr = min(ROW_CHUNK, rows - r)
            for s, ks in taps.items():
                acc = None
                for k in ks:
                    r0 = base + k - s + r
                    term = ext_ref[r0:r0 + nr, c0 + cc:c0 + cc + cch] * w_ref[k:k + 1, c0 + cc:c0 + cc + cch]
                    acc = term if acc is None else acc + term
                zs_ref[s, r:r + nr, cc:cc + cch] = acc
    for cc in range(0, cb, cch):
        for r in range(0, tt, ROW_CHUNK):
            nr = min(ROW_CHUNK, tt - r)
            out = None
            for s in taps:
                part = zs_ref[s, s + r:s + r + nr, cc:cc + cch]
                out = part if out is None else out + part
            if bias_ref is not None:
                out = out + bias_ref[:, c0 + cc:c0 + cc + cch]
            out_ref[r:r + nr, c0 + cc:c0 + cc + cch] = out


def _carry_ext(ext_ref, cache_ref, hb, tt):
    @pl.when(pl.program_id(1) == 0)
    def _():
        ext_ref[0:hb, :] = cache_ref[0]
        ext_ref[hb + tt:hb + tt + SUBLANES, :] = jnp.zeros((SUBLANES, ext_ref.shape[1]), F32)

    @pl.when(pl.program_id(1) > 0)
    def _():
        ext_ref[0:hb, :] = ext_ref[tt:tt + hb, :]


def _conformer_kernel(x_ref, gpre_ref, w1_ref, b1_ref, cache_ref, dw_ref, dwb_ref, lng_ref, lnb_ref, w2_ref,
                      b2_ref, gpost_ref, o_ref, tail_ref, ext_ref, zs_ref, h_ref, c_ref, y_ref,
                      *, width, hb, tt, d, cb):
    _carry_ext(ext_ref, cache_ref, hb, tt)
    rch = min(ROW_CHUNK, tt)
    for r in range(0, tt, rch):
        h_ref[r:r + rch, :] = _rms(x_ref[0, r:r + rch, :], gpre_ref[...]).astype(BF16)
    h = h_ref[...]
    for c0 in range(0, d, cb):
        a = jnp.dot(h, w1_ref[:, c0:c0 + cb], preferred_element_type=F32) + b1_ref[:, c0:c0 + cb]
        g = jnp.dot(h, w1_ref[:, d + c0:d + c0 + cb], preferred_element_type=F32) + b1_ref[:, d + c0:d + c0 + cb]
        ext_ref[hb:hb + tt, c0:c0 + cb] = a * jax.nn.sigmoid(g)
        _dwconv_block(ext_ref, zs_ref, dw_ref, dwb_ref, c_ref, width, hb, tt, c0, cb)
    for r in range(0, tt, rch):
        c = c_ref[r:r + rch, :]
        cc = c - jnp.mean(c, axis=-1, keepdims=True)
        y = cc * lax.rsqrt(jnp.mean(cc * cc, axis=-1, keepdims=True) + EPS) * lng_ref[...] + lnb_ref[...]
        y_ref[r:r + rch, :] = _silu(y).astype(BF16)
    out = jnp.dot(y_ref[...], w2_ref[...], preferred_element_type=F32)
    for r in range(0, tt, rch):
        o_ref[0, r:r + rch, :] = x_ref[0, r:r + rch, :] + _rms(out[r:r + rch] + b2_ref[...], gpost_ref[...])
    tail_ref[0] = ext_ref[tt:tt + hb, :]


def _const_spec(shape, single=False):
    nd = len(shape)
    if single:
        return pl.BlockSpec(shape, lambda b, i: (0,) * nd, pipeline_mode=pl.Buffered(1))
    return pl.BlockSpec(shape, lambda b, i: (0,) * nd)


def conformer_mixer(x, gpre, w1, b1, cache, dw, dwb, lng, lnb, w2, b2, gpost, tt=256, cb=512):
    bsz, t, d = x.shape
    width = dw.shape[0]
    hb = 32
    tt = _pick_tile(t, tt)
    cb = _pick_tile(d, cb)
    row = lambda a: a.reshape(1, -1)
    xspec = pl.BlockSpec((1, tt, d), lambda b, i: (b, i, 0))
    out, tail = pl.pallas_call(
        functools.partial(_conformer_kernel, width=width, hb=hb, tt=tt, d=d, cb=cb),
        grid=(bsz, t // tt),
        in_specs=[xspec, _const_spec((1, d)), _const_spec((d, 2 * d), single=True), _const_spec((1, 2 * d)),
                  pl.BlockSpec((1, hb, d), lambda b, i: (b, 0, 0)), _const_spec((hb, d)), _const_spec((1, d)),
                  _const_spec((1, d)), _const_spec((1, d)), _const_spec((d, d), single=True),
                  _const_spec((1, d)), _const_spec((1, d))],
        out_specs=[xspec, pl.BlockSpec((1, hb, d), lambda b, i: (b, 0, 0))],
        out_shape=[jax.ShapeDtypeStruct((bsz, t, d), F32), jax.ShapeDtypeStruct((bsz, hb, d), F32)],
        scratch_shapes=[pltpu.VMEM((hb + tt + SUBLANES, d), F32), pltpu.VMEM((SUBLANES, tt + SUBLANES, cb), F32),
                        pltpu.VMEM((tt, d), BF16), pltpu.VMEM((tt, d), F32), pltpu.VMEM((tt, d), BF16)],
        compiler_params=_cparams(("parallel", "arbitrary")),
        name="conformer_mixer",
    )(x, row(gpre), w1, row(b1), _pad_rows(cache, hb), jnp.pad(dw, ((0, hb - width), (0, 0))), row(dwb),
      row(lng), row(lnb), w2, row(b2), row(gpost))
    return out, tail[:, hb - (width - 1):]


def _sconv_tail_kernel(gx_ref, cache_ref, bg_ref, cw_ref, w_ref, gpost_ref, x_ref, o_ref, ext_ref, zs_ref, c_ref,
                       y_ref, *, width, hb, tt, d, cb):
    _carry_ext(ext_ref, cache_ref, hb, tt)
    ext_ref[hb:hb + tt, :] = gx_ref[0]
    _dwconv(ext_ref, zs_ref, cw_ref, c_ref, width, hb, tt, d, cb)
    rch = min(ROW_CHUNK, tt)
    for r in range(0, tt, rch):
        y_ref[r:r + rch, :] = (bg_ref[0, r:r + rch, :] * c_ref[r:r + rch, :]).astype(BF16)
    out = jnp.dot(y_ref[...], w_ref[...], preferred_element_type=F32)
    for r in range(0, tt, rch):
        o_ref[0, r:r + rch, :] = x_ref[0, r:r + rch, :] + _rms(out[r:r + rch], gpost_ref[...])


def sconv_tail(gx, cache, bg, cw, w_out, gpost, x, tt=256, cb=512):
    bsz, t, d = gx.shape
    width = cw.shape[0]
    hb = SUBLANES
    tt = _pick_tile(t, tt)
    cb = _pick_tile(d, cb)
    cur = pl.BlockSpec((1, tt, d), lambda b, i: (b, i, 0))
    return pl.pallas_call(
        functools.partial(_sconv_tail_kernel, width=width, hb=hb, tt=tt, d=d, cb=cb),
        grid=(bsz, t // tt),
        in_specs=[cur, pl.BlockSpec((1, hb, d), lambda b, i: (b, 0, 0)), cur, _const_spec((hb, d)),
                  _const_spec((d, d), single=True), _const_spec((1, d)), cur],
        out_specs=cur,
        out_shape=jax.ShapeDtypeStruct((bsz, t, d), F32),
        scratch_shapes=[pltpu.VMEM((hb + tt + SUBLANES, d), F32), pltpu.VMEM((SUBLANES, tt + SUBLANES, cb), F32),
                        pltpu.VMEM((tt, d), F32), pltpu.VMEM((tt, d), BF16)],
        compiler_params=_cparams(("parallel", "arbitrary")),
        name="sconv_tail",
    )(gx, _pad_rows(cache, hb), bg, jnp.pad(cw, ((0, hb - width), (0, 0))), w_out, gpost.reshape(1, d), x)


def _delta_kernel(*refs, hp, L, has_state):
    hd = DN_HEAD_DIM
    if has_state:
        (q_ref, k_ref, v_ref, z_ref, ba_ref, cq_ref, ck_ref, cv_ref, wq_ref, wk_ref, wv_ref, par_ref, ng_ref,
         s0_ref, o_ref, s_ref, eq_ref, ek_ref, ev_ref) = refs
    else:
        (q_ref, k_ref, v_ref, z_ref, ba_ref, cq_ref, ck_ref, cv_ref, wq_ref, wk_ref, wv_ref, par_ref, ng_ref,
         o_ref, s_ref, eq_ref, ek_ref, ev_ref) = refs
        s0_ref = None
    hb = SUBLANES
    width = 4
    L2 = 2 * L

    @pl.when(pl.program_id(2) == 0)
    def _():
        eq_ref[0:hb, :] = cq_ref[0]
        ek_ref[0:hb, :] = ck_ref[0]
        ev_ref[0:hb, :] = cv_ref[0]
        if has_state:
            s_ref[...] = s0_ref[...]
        else:
            s_ref[...] = jnp.zeros(s_ref.shape, F32)

    eq_ref[hb:hb + L, :] = q_ref[0]
    ek_ref[hb:hb + L, :] = k_ref[0]
    ev_ref[hb:hb + L, :] = v_ref[0]

    def conv_silu(e_ref, w_ref, lo):
        acc = None
        for kk in range(width):
            r0 = hb - (width - 1) + kk
            term = e_ref[r0:r0 + L, lo:lo + hd] * w_ref[kk:kk + 1, lo:lo + hd]
            acc = term if acc is None else acc + term
        return _silu(acc)

    ba = ba_ref[0]
    beta_all = jax.nn.sigmoid(ba)
    g_all = -jnp.exp(par_ref[0, 0:1, :]) * jax.nn.softplus(ba + par_ref[0, 1:2, :])
    ri = lax.broadcasted_iota(jnp.int32, (L, L), 0)
    ci = lax.broadcasted_iota(jnp.int32, (L, L), 1)
    gc_all = _mm_f32(jnp.where(ri >= ci, 1.0, 0.0).astype(F32), g_all)
    gc_t = gc_all.T
    grow_all = jnp.concatenate([gc_t[64:64 + hp], gc_t[96:96 + hp]], axis=1)
    lane_h = lax.broadcasted_iota(jnp.int32, (hp, L2), 1)
    gl_e = jnp.broadcast_to(grow_all[:, L - 1:L], (hp, L2))
    gl_o = jnp.broadcast_to(grow_all[:, L2 - 1:L2], (hp, L2))
    kdec_all = jnp.exp(jnp.where(lane_h < L, gl_e, gl_o) - grow_all)
    egl_e = jnp.exp(gl_e)
    egl_o = jnp.exp(gl_o)

    r2 = lax.broadcasted_iota(jnp.int32, (L, L2), 0)
    l2 = lax.broadcasted_iota(jnp.int32, (L, L2), 1)
    c2 = jnp.bitwise_and(l2, L - 1)
    incl2 = r2 >= c2
    strict2 = r2 > c2
    first2 = l2 < L
    eye2 = jnp.where(r2 == c2, 1.0, 0.0).astype(F32)
    rb = lax.broadcasted_iota(jnp.int32, (L2, L2), 0)
    lb = lax.broadcasted_iota(jnp.int32, (L2, L2), 1)
    bmask = (rb < L) == (lb < L)
    zero_sq = jnp.zeros((L, hd), F32)

    def bdiag(x):
        xb = x.astype(BF16)
        return jnp.where(bmask, jnp.concatenate([xb, xb], axis=0), jnp.zeros((), BF16))

    def bcast(col):
        return jnp.broadcast_to(col, (L, hd))

    pairs = range(hp)
    qn, kn, knt2 = [], [], []
    for j in pairs:
        qj = conv_silu(eq_ref, wq_ref, j * hd)
        kj = conv_silu(ek_ref, wk_ref, j * hd)
        qn.append(qj * lax.rsqrt(jnp.sum(qj * qj, axis=-1, keepdims=True) + EPS) * (hd ** -0.5))
        kn.append(kj * lax.rsqrt(jnp.sum(kj * kj, axis=-1, keepdims=True) + EPS))
        knt2.append(jnp.concatenate([kn[j], kn[j]], axis=0).T)

    a2 = [_mm(jnp.concatenate([qn[j], kn[j]], axis=0), knt2[j]) for j in pairs]

    bcs, egcs, tmat, pmat, qkm2 = [], [], [], [], []
    for j in pairs:
        gce, gco = bcast(gc_all[:, 64 + j:65 + j]), bcast(gc_all[:, 96 + j:97 + j])
        bce, bco = bcast(beta_all[:, j:j + 1]), bcast(beta_all[:, 32 + j:33 + j])
        bcs.append((bce, bco))
        egcs.append((jnp.exp(gce), jnp.exp(gco)))
        gcol2 = jnp.where(first2, gce, gco)
        bcol2 = jnp.where(first2, bce, bco)
        decay2 = jnp.exp(jnp.where(incl2, gcol2 - grow_all[j:j + 1, :], NEG_INF))
        m2 = jnp.where(strict2, bcol2 * a2[j][L:] * decay2, 0.0)
        qkm2.append(jnp.where(incl2, a2[j][:L] * decay2, 0.0))
        tmat.append(eye2 - m2)
        pmat.append(m2)

    nst = int(math.log2(L)) - 1
    pmat = [_mm(pmat[j], bdiag(pmat[j])) for j in pairs]
    for st in range(nst):
        if st < nst - 1:
            outs = [_mm(jnp.concatenate([tmat[j], pmat[j]], axis=0), bdiag(pmat[j])) for j in pairs]
            tmat = [tmat[j] + outs[j][:L] for j in pairs]
            pmat = [outs[j][L:] for j in pairs]
        else:
            tmat = [tmat[j] + _mm(tmat[j], bdiag(pmat[j])) for j in pairs]

    heads = [(j, r) for j in pairs for r in range(2)]
    sols = []
    for j, r in heads:
        h = 2 * j + r
        vh = conv_silu(ev_ref, wv_ref, h * hd)
        bc_, egc_ = bcs[j][r], egcs[j][r]
        rhs = jnp.concatenate([bc_ * vh, (bc_ * egc_) * kn[j]], axis=1).astype(BF16)
        zr = jnp.zeros_like(rhs)
        rhs_pad = jnp.concatenate([rhs, zr] if r == 0 else [zr, rhs], axis=0)
        sols.append(_mm(tmat[j], rhs_pad))

    xs = []
    for idx, (j, r) in enumerate(heads):
        h = 2 * j + r
        w_ = sols[idx][:, hd:]
        xs.append(_mm(jnp.concatenate([w_, qn[j] * egcs[j][r]], axis=0), s_ref[0, h]))

    kgt2 = [knt2[j] * kdec_all[j:j + 1, :] for j in pairs]
    for idx, (j, r) in enumerate(heads):
        h = 2 * j + r
        v_new = sols[idx][:, :hd] - xs[idx][:L]
        vpad = jnp.concatenate([v_new, zero_sq] if r == 0 else [zero_sq, v_new], axis=0)
        y = _mm(jnp.concatenate([qkm2[j], kgt2[j]], axis=0), vpad)
        egl = (egl_e if r == 0 else egl_o)[j:j + 1, :]
        s_ref[0, h] = s_ref[0, h] * egl + y[L:]
        o = xs[idx][L:] + y[:L]
        zh = z_ref[0, :, h * hd:(h + 1) * hd]
        o = o * lax.rsqrt(jnp.mean(o * o, axis=-1, keepdims=True) + EPS) * ng_ref[...] * _silu(zh)
        o_ref[0, :, h * hd:(h + 1) * hd] = o.astype(o_ref.dtype)

    eq_ref[0:hb, :] = eq_ref[L:L + hb, :]
    ek_ref[0:hb, :] = ek_ref[L:L + hb, :]
    ev_ref[0:hb, :] = ev_ref[L:L + hb, :]


def delta_rule(qkv, z, ba, conv_cache, conv_w, par, norm_g, s0, hp):
    bsz, t, _ = qkv.shape
    vdim = z.shape[-1]
    hd = DN_HEAD_DIM
    assert 2 * CHUNK == hd, "head pairs are packed into one lane tile"
    vh = vdim // hd
    qh = vh // 2
    hg = qh // hp
    nh = 2 * hp
    L = CHUNK
    nc = t // L
    hb = SUBLANES
    cache = _pad_rows(conv_cache, hb)
    cw = jnp.pad(conv_w, ((0, hb - conv_w.shape[0]), (0, 0)))
    qw, vw = hp * hd, nh * hd
    koff, voff = qh // hp, 2 * qh // nh
    in_specs = [
        pl.BlockSpec((1, L, qw), lambda b, g, c: (b, c, g)),
        pl.BlockSpec((1, L, qw), lambda b, g, c: (b, c, koff + g)),
        pl.BlockSpec((1, L, vw), lambda b, g, c: (b, c, voff + g)),
        pl.BlockSpec((1, L, vw), lambda b, g, c: (b, c, g)),
        pl.BlockSpec((1, L, LANES), lambda b, g, c: (b, c, g)),
        pl.BlockSpec((1, hb, qw), lambda b, g, c: (b, 0, g)),
        pl.BlockSpec((1, hb, qw), lambda b, g, c: (b, 0, koff + g)),
        pl.BlockSpec((1, hb, vw), lambda b, g, c: (b, 0, voff + g)),
        pl.BlockSpec((hb, qw), lambda b, g, c: (0, g)),
        pl.BlockSpec((hb, qw), lambda b, g, c: (0, koff + g)),
        pl.BlockSpec((hb, vw), lambda b, g, c: (0, voff + g)),
        pl.BlockSpec((1, hb, LANES), lambda b, g, c: (g, 0, 0)),
        pl.BlockSpec((1, hd), lambda b, g, c: (0, 0)),
    ]
    args = [qkv, qkv, qkv, z, ba, cache, cache, cache, cw, cw, cw, par, norm_g.reshape(1, hd)]
    if s0 is not None:
        in_specs.append(pl.BlockSpec((1, nh, hd, hd), lambda b, g, c: (b, g, 0, 0)))
        args.append(s0)
    o, s = pl.pallas_call(
        functools.partial(_delta_kernel, hp=hp, L=L, has_state=s0 is not None),
        grid=(bsz, hg, nc),
        in_specs=in_specs,
        out_specs=[pl.BlockSpec((1, L, vw), lambda b, g, c: (b, c, g)),
                   pl.BlockSpec((1, nh, hd, hd), lambda b, g, c: (b, g, 0, 0))],
        out_shape=[jax.ShapeDtypeStruct((bsz, t, vdim), BF16),
                   jax.ShapeDtypeStruct((bsz, vh, hd, hd), F32)],
        scratch_shapes=[pltpu.VMEM((hb + L, qw), F32), pltpu.VMEM((hb + L, qw), F32),
                        pltpu.VMEM((hb + L, vw), F32)],
        compiler_params=_cparams(("parallel", "parallel", "arbitrary")),
        name="delta_rule",
    )(*args)
    return o, s


def _attn_kernel(q_ref, k0_ref, k1_ref, k2_ref, v0_ref, v1_ref, v2_ref, bias_ref, o_ref, *, nkv, grp, masked):
    hd = SWA_HEAD_DIM
    pw = 2 * hd
    npair = grp // 2
    nq = q_ref.shape[1]
    nk = k0_ref.shape[1] * 3
    nkp = bias_ref.shape[-1]
    c = pl.program_id(1)
    q = q_ref[0]
    zrows = jnp.zeros((nkp - nk, k0_ref.shape[2]), F32)
    k = jnp.concatenate([k0_ref[0], k1_ref[0], k2_ref[0], zrows], axis=0)
    v = jnp.concatenate([v0_ref[0], v1_ref[0], v2_ref[0], zrows], axis=0)
    low = lax.broadcasted_iota(jnp.int32, (nkp, pw), 1) < hd
    if masked:
        col = lax.broadcasted_iota(jnp.int32, (npair * nq, nkp), 1)
        valid = jnp.logical_or(col >= nk, c * CHUNK - WINDOW + col >= 0)
    ones = jnp.ones((nkp, pw), BF16)

    def halves(x, n):
        blk = x[:, (n // 2) * pw:(n // 2 + 1) * pw]
        swp = pltpu.roll(blk, hd, axis=1)
        lo_src, hi_src = (blk, swp) if n % 2 == 0 else (swp, blk)
        return (jnp.where(low, lo_src, 0.0).astype(BF16), jnp.where(low, 0.0, hi_src).astype(BF16))

    heads = range(nkv)
    kh = [halves(k, n) for n in heads]
    vh = [halves(v, n) for n in heads]
    q2 = [jnp.concatenate([q[:, (n * grp + 2 * a) * hd:(n * grp + 2 * a + 2) * hd] for a in range(npair)],
                          axis=0).astype(BF16) for n in heads]
    units = [(n, par) for n in heads for par in range(2)]
    ss = []
    for n, par in units:
        s = _mm_nt(q2[n], kh[n][par]) * (hd ** -0.5) + bias_ref[n, par]
        ss.append(jnp.where(valid, s, NEG_INF) if masked else s)
    ps = [jnp.exp(s - jnp.max(s, axis=-1, keepdims=True)).astype(BF16) for s in ss]
    dens = [jnp.dot(p, ones, preferred_element_type=F32) for p in ps]
    avs = [jnp.dot(ps[i], vh[n][par], preferred_element_type=F32) for i, (n, par) in enumerate(units)]
    for n in heads:
        o2 = avs[2 * n] / dens[2 * n] + avs[2 * n + 1] / dens[2 * n + 1]
        for a in range(npair):
            o_ref[0, :, (n * grp + 2 * a) * hd:(n * grp + 2 * a + 2) * hd] = o2[a * nq:(a + 1) * nq].astype(o_ref.dtype)


def _t5_bucket(rel):
    half = REL_BUCKETS // 2
    max_exact = half // 2
    a = jnp.abs(rel)
    af = jnp.maximum(a, 1).astype(F32)
    large = max_exact + (jnp.log(af / max_exact) / math.log(REL_MAX_DIST / max_exact)
                         * (half - max_exact)).astype(jnp.int32)
    large = jnp.minimum(large, half - 1)
    return jnp.where(rel > 0, half, 0) + jnp.where(a < max_exact, a, large)


def _bias_table(rel_bias, sinks, nkv, grp, n_q, n_k, n_kp):
    npair = grp // 2
    rel = jnp.arange(n_k)[None, :] - WINDOW - jnp.arange(n_q)[:, None]
    bias = jnp.take(rel_bias, _t5_bucket(rel), axis=0).astype(F32)
    bias = jnp.transpose(bias, (2, 0, 1)).reshape(nkv, npair, 2, n_q, n_k)
    bias = jnp.transpose(bias, (0, 2, 1, 3, 4)).reshape(nkv, 2, npair * n_q, n_k)
    sink = jnp.transpose(sinks.astype(F32).reshape(nkv, npair, 2), (0, 2, 1))
    sink = jnp.repeat(sink, n_q, axis=2).reshape(nkv, 2, npair * n_q, 1)
    pad = jnp.full((nkv, 2, npair * n_q, n_kp - n_k - 1), NEG_INF, F32)
    return jnp.concatenate([bias, sink, pad], axis=-1)


def swa_attention(q_src, k_src, v_src, kv_col, rel_bias, sinks, nkv, masked):
    hd = SWA_HEAD_DIM
    nheads = sinks.shape[0]
    grp = nheads // nkv
    assert grp % 2 == 0 and nkv % 2 == 0, "heads are processed in lane-tile pairs"
    qd, kvd = nheads * hd, nkv * hd
    bsz = q_src.shape[0]
    t = k_src.shape[1] if masked else k_src.shape[1] - WINDOW
    nc = t // CHUNK
    nkp = 2 * LANES
    bias = _bias_table(rel_bias, sinks, nkv, grp, CHUNK, WINDOW + CHUNK, nkp)
    if masked:
        rows = [lambda b, c, j=j: jnp.maximum(c + j - 2, 0) for j in range(3)]
    else:
        rows = [lambda b, c, j=j: c + j for j in range(3)]
    kspecs = [pl.BlockSpec((1, CHUNK, kvd), lambda b, c, r=r: (b, r(b, c), kv_col[0])) for r in rows]
    vspecs = [pl.BlockSpec((1, CHUNK, kvd), lambda b, c, r=r: (b, r(b, c), kv_col[1])) for r in rows]
    return pl.pallas_call(
        functools.partial(_attn_kernel, nkv=nkv, grp=grp, masked=masked),
        grid=(bsz, nc),
        in_specs=[pl.BlockSpec((1, CHUNK, qd), lambda b, c: (b, c, 0))] + kspecs + vspecs
        + [pl.BlockSpec((nkv, 2, grp // 2 * CHUNK, nkp), lambda b, c: (0, 0, 0, 0))],
        out_specs=pl.BlockSpec((1, CHUNK, qd), lambda b, c: (b, c, 0)),
        out_shape=jax.ShapeDtypeStruct((bsz, t, qd), BF16),
        compiler_params=_cparams(("parallel", "arbitrary")),
        name="swa_attention",
    )(q_src, k_src, k_src, k_src, v_src, v_src, v_src, bias)


def _delta_heads_per_step(qh):
    return min(16, qh)


def _run_group(x, caches, w, first_chunk):
    conv_a, delta_s, delta_conv, sconv, swa_k, swa_v = caches
    bsz, t, d = x.shape
    depth = w["norm_mix_pre"].shape[0]
    xf = x.reshape(bsz * t, d)
    new = {k: [] for k in ("conv_a", "ds", "dc", "sc", "k", "v")}
    bf = lambda a: a.astype(BF16)
    for i in range(depth):
        mix, j = i % 4, i // 4
        g_pre, g_post = w["norm_mix_pre"][i], w["norm_mix_post"][i]
        if mix == 0:
            width = w["conv_a_dw"].shape[1]
            cache = conv_a[j] if conv_a is not None else jnp.zeros((bsz, width - 1, d), F32)
            x3, tail = conformer_mixer(xf.reshape(bsz, t, d), g_pre, bf(w["conv_a_w1"][j]), w["conv_a_b1"][j], cache,
                                       w["conv_a_dw"][j], w["conv_a_dw_b"][j], w["conv_a_ln_g"][j],
                                       w["conv_a_ln_b"][j], bf(w["conv_a_w2"][j]), w["conv_a_b2"][j], g_post)
            xf = x3.reshape(bsz * t, d)
            new["conv_a"].append(tail)
        elif mix == 1:
            w_in = w["delta_w_in"][j]
            vh = w["delta_a_log"].shape[1]
            vdim = vh * DN_HEAD_DIM
            qkvd = w["delta_conv_w"].shape[2]
            qh = (qkvd - vdim) // (2 * DN_HEAD_DIM)
            hp = _delta_heads_per_step(qh)
            hg = qh // hp
            (qkv,) = norm_matmul(xf, g_pre, [bf(w_in[:, :qkvd])], None, _comb_id, [F32])
            (z,) = norm_matmul(xf, g_pre, [bf(w_in[:, qkvd:qkvd + vdim])], None, _comb_id, [F32])

            def lanes4(be, bo, ae, ao):
                pad = [(0, 0)] * (be.ndim - 1) + [(0, 32 - hp)]
                return jnp.concatenate([jnp.pad(a, pad) for a in (be, bo, ae, ao)], axis=-1)

            wb = w_in[:, qkvd + vdim:qkvd + vdim + vh].reshape(d, hg, hp, 2)
            wa = w_in[:, qkvd + vdim + vh:].reshape(d, hg, hp, 2)
            w_ba = lanes4(wb[..., 0], wb[..., 1], wa[..., 0], wa[..., 1]).reshape(d, hg * LANES)
            (ba,) = norm_matmul(xf, g_pre, [bf(w_ba)], None, _comb_id, [F32], tn=LANES)
            zl = jnp.zeros((hg, hp), F32)
            alog = w["delta_a_log"][j].reshape(hg, hp, 2)
            dtb = w["delta_dt_bias"][j].reshape(hg, hp, 2)
            par = jnp.stack([lanes4(zl, zl, alog[..., 0], alog[..., 1]),
                             lanes4(zl, zl, dtb[..., 0], dtb[..., 1])], axis=1)
            par = jnp.pad(par, ((0, 0), (0, SUBLANES - 2), (0, 0)))
            cw = w["delta_conv_w"][j]
            cache = delta_conv[j] if delta_conv is not None else jnp.zeros((bsz, cw.shape[0] - 1, qkvd), F32)
            qkv3 = qkv.reshape(bsz, t, qkvd)
            o, s_new = delta_rule(qkv3, z.reshape(bsz, t, vdim), ba.reshape(bsz, t, hg * LANES), cache, cw, par,
                                  w["delta_norm_g"][j], delta_s[j] if delta_s is not None else None, hp)
            xf = matmul_norm_residual(o.reshape(bsz * t, vdim), bf(w["delta_w_out"][j]), g_post, xf)
            new["ds"].append(s_new)
            new["dc"].append(qkv3[:, -(cw.shape[0] - 1):])
        elif mix == 2:
            w_in = bf(w["sconv_w_in"][j])
            bg, gx = norm_matmul(xf, g_pre, [w_in[:, :d], w_in[:, d:2 * d], w_in[:, 2 * d:]], None, _comb_sconv,
                                 [F32, F32], tm=512, tn=512)
            gx = gx.reshape(bsz, t, d)
            cw = w["sconv_w"][j]
            cache = sconv[j] if sconv is not None else jnp.zeros((bsz, cw.shape[0] - 1, d), F32)
            xf = sconv_tail(gx, cache, bg.reshape(bsz, t, d), cw, bf(w["sconv_w_out"][j]), g_post,
                            xf.reshape(bsz, t, d)).reshape(bsz * t, d)
            new["sc"].append(gx[:, -(cw.shape[0] - 1):])
        else:
            w_qkv = w["swa_w_qkv"][j]
            nheads = w["swa_sinks"].shape[1]
            qd = nheads * SWA_HEAD_DIM
            kvd = (w_qkv.shape[1] - qd) // 2
            nkv = kvd // SWA_HEAD_DIM
            (qkv,) = norm_matmul(xf, g_pre, [bf(w_qkv)], None, _comb_id, [F32])
            qkv = qkv.reshape(bsz, t, qd + 2 * kvd)
            if first_chunk:
                o = swa_attention(qkv, qkv, qkv, (qd // kvd, qd // kvd + 1), w["rel_bias"], w["swa_sinks"][j],
                                  nkv, True)
                k_ext, v_ext = qkv[:, :, qd:qd + kvd], qkv[:, :, qd + kvd:]
            else:
                k_ext = jnp.concatenate([swa_k[j].reshape(bsz, WINDOW, kvd), qkv[:, :, qd:qd + kvd]], axis=1)
                v_ext = jnp.concatenate([swa_v[j].reshape(bsz, WINDOW, kvd), qkv[:, :, qd + kvd:]], axis=1)
                o = swa_attention(qkv, k_ext, v_ext, (0, 0), w["rel_bias"], w["swa_sinks"][j], nkv, False)
            xf = matmul_norm_residual(o.reshape(bsz * t, qd), bf(w["swa_w_out"][j]), g_post, xf)
            new["k"].append(k_ext[:, -WINDOW:].reshape(bsz, WINDOW, nkv, SWA_HEAD_DIM))
            new["v"].append(v_ext[:, -WINDOW:].reshape(bsz, WINDOW, nkv, SWA_HEAD_DIM))
        xf = ffn(xf, w["norm_ffn_pre"][i], bf(w["ffn_w_gate"][i]), bf(w["ffn_w_up"][i]), bf(w["ffn_w_down"][i]),
                 w["norm_ffn_post"][i])
    return xf.reshape(bsz, t, d), tuple(jnp.stack(new[k]) for k in ("conv_a", "ds", "dc", "sc", "k", "v"))


def kernel(x_prompt, x_sample, cache_conv_a, state_delta_s, state_delta_conv, cache_sconv, cache_swa_k, cache_swa_v, rel_bias, norm_mix_pre, norm_mix_post, norm_ffn_pre, norm_ffn_post, ffn_w_gate, ffn_w_up, ffn_w_down, conv_a_w1, conv_a_b1, conv_a_dw, conv_a_dw_b, conv_a_ln_g, conv_a_ln_b, conv_a_w2, conv_a_b2, delta_w_in, delta_conv_w, delta_a_log, delta_dt_bias, delta_norm_g, delta_w_out, sconv_w_in, sconv_w, sconv_w_out, swa_w_qkv, swa_sinks, swa_w_out):
    w = {
        "rel_bias": rel_bias, "norm_mix_pre": norm_mix_pre, "norm_mix_post": norm_mix_post,
        "norm_ffn_pre": norm_ffn_pre, "norm_ffn_post": norm_ffn_post, "ffn_w_gate": ffn_w_gate,
        "ffn_w_up": ffn_w_up, "ffn_w_down": ffn_w_down, "conv_a_w1": conv_a_w1, "conv_a_b1": conv_a_b1,
        "conv_a_dw": conv_a_dw, "conv_a_dw_b": conv_a_dw_b, "conv_a_ln_g": conv_a_ln_g,
        "conv_a_ln_b": conv_a_ln_b, "conv_a_w2": conv_a_w2, "conv_a_b2": conv_a_b2,
        "delta_w_in": delta_w_in, "delta_conv_w": delta_conv_w, "delta_a_log": delta_a_log,
        "delta_dt_bias": delta_dt_bias, "delta_norm_g": delta_norm_g, "delta_w_out": delta_w_out,
        "sconv_w_in": sconv_w_in, "sconv_w": sconv_w, "sconv_w_out": sconv_w_out,
        "swa_w_qkv": swa_w_qkv, "swa_sinks": swa_sinks, "swa_w_out": swa_w_out,
    }
    y_p, (ca_p, ds_p, dc_p, sc_p, k_p, v_p) = _run_group(x_prompt, (None,) * 6, w, True)
    y_s, (ca_s, ds_s, dc_s, sc_s, k_s, v_s) = _run_group(
        x_sample, (cache_conv_a, state_delta_s, state_delta_conv, cache_sconv, cache_swa_k, cache_swa_v), w, False)
    return (y_p, y_s, ca_p, ca_s, ds_p, ds_s, dc_p, dc_s, sc_p, sc_s, k_p, k_s, v_p, v_s)
```

```python
import functools
import math

import jax
import jax.numpy as jnp
from jax import lax
from jax.experimental import pallas as pl
from jax.experimental.pallas import tpu as pltpu

F32 = jnp.float32
BF16 = jnp.bfloat16
EPS = 1e-6
CHUNK = 64
WINDOW = 128
SWA_HEAD_DIM = 64
DN_HEAD_DIM = 128
REL_BUCKETS = 32
REL_MAX_DIST = 128
V7X_VMEM_BUDGET = 56 * 1024 * 1024
SUBLANES = 8
LANES = 128
NEG_INF = float("-inf")


def _cparams(sem, vmem=V7X_VMEM_BUDGET):
    return pltpu.CompilerParams(dimension_semantics=sem, vmem_limit_bytes=vmem)


def _rms(x, g):
    return x * lax.rsqrt(jnp.mean(x * x, axis=-1, keepdims=True) + EPS) * g


def _silu(x):
    return x * jax.nn.sigmoid(x)


def _mm(a, b):
    return jnp.dot(a.astype(BF16), b.astype(BF16), preferred_element_type=F32)


def _mm_nt(a, b):
    return lax.dot_general(a.astype(BF16), b.astype(BF16), (((1,), (1,)), ((), ())),
                           preferred_element_type=F32)


def _mm_f32(a, b):
    return jnp.dot(a, b, preferred_element_type=F32, precision=lax.Precision.HIGHEST)


def _pick_tile(n, pref):
    t = min(n, pref)
    while n % t:
        t //= 2
    return t


def _pad_rows(a, rows):
    pad = rows - a.shape[-2]
    cfg = [(0, 0)] * a.ndim
    cfg[-2] = (pad, 0)
    return jnp.pad(a, cfg)


def _mm_cols_kernel(*refs, n_w, combine, n_out):
    h = refs[0][...]
    ds = [jnp.dot(h, refs[1 + i][...], preferred_element_type=F32) for i in range(n_w)]
    for o_ref, o in zip(refs[1 + n_w:1 + n_w + n_out], combine(*ds)):
        o_ref[...] = o.astype(o_ref.dtype)


def matmul_cols(h, ws, combine, out_dtypes, tm=2048, tn=1024):
    t, k = h.shape
    n = ws[0].shape[1]
    tm = _pick_tile(t, tm)
    tn = _pick_tile(n, tn)
    return pl.pallas_call(
        functools.partial(_mm_cols_kernel, n_w=len(ws), combine=combine, n_out=len(out_dtypes)),
        grid=(t // tm, n // tn),
        in_specs=[pl.BlockSpec((tm, k), lambda i, j: (i, 0))] + [pl.BlockSpec((k, tn), lambda i, j: (0, j)) for _ in ws],
        out_specs=[pl.BlockSpec((tm, tn), lambda i, j: (i, j)) for _ in out_dtypes],
        out_shape=[jax.ShapeDtypeStruct((t, n), dt) for dt in out_dtypes],
        compiler_params=_cparams(("parallel", "arbitrary")),
        name="matmul_cols",
    )(h, *ws)


def _comb_id(d):
    return (d,)


def _comb_sconv(bg, cg, xin):
    return (bg, cg * xin)


def _mm_norm_res_kernel(a_ref, w_ref, g_ref, gn_ref, x_ref, o_ref, hn_ref):
    d = jnp.dot(a_ref[...], w_ref[...], preferred_element_type=F32)
    o = x_ref[...] + _rms(d, g_ref[...])
    o_ref[...] = o
    hn_ref[...] = _rms(o, gn_ref[...]).astype(BF16)


def matmul_norm_residual(a, w, g, x, g_next, tm=512):
    t, k = a.shape
    d = w.shape[1]
    tm = _pick_tile(t, tm)
    row = pl.BlockSpec((tm, d), lambda i: (i, 0))
    return pl.pallas_call(
        _mm_norm_res_kernel,
        grid=(t // tm,),
        in_specs=[pl.BlockSpec((tm, k), lambda i: (i, 0)),
                  pl.BlockSpec((k, d), lambda i: (0, 0), pipeline_mode=pl.Buffered(1)),
                  pl.BlockSpec((1, d), lambda i: (0, 0)),
                  pl.BlockSpec((1, d), lambda i: (0, 0)),
                  row],
        out_specs=[row, row],
        out_shape=[jax.ShapeDtypeStruct((t, d), F32), jax.ShapeDtypeStruct((t, d), BF16)],
        compiler_params=_cparams(("parallel",)),
        name="matmul_norm_residual",
    )(a, w, g.reshape(1, d), g_next.reshape(1, d), x)


def _ffn_kernel(*refs, n, nf, nchunk, dchunk, emit_next):
    if emit_next:
        h_ref, x_ref, wg_ref, wu_ref, wd_ref, g2_ref, gn_ref, o_ref, hn_ref, acc_ref = refs
    else:
        h_ref, x_ref, wg_ref, wu_ref, wd_ref, g2_ref, o_ref, acc_ref = refs
    i = pl.program_id(0)
    f = pl.program_id(1)
    slot = lax.rem(i, 2)
    tm, d = acc_ref.shape[1], acc_ref.shape[2]
    rc = tm // nchunk

    @pl.when(jnp.logical_and(i == 0, f == 0))
    def _():
        acc_ref[...] = jnp.zeros(acc_ref.shape, F32)

    def epilogue():
        r0 = pl.multiple_of(jnp.minimum(f, nchunk - 1) * rc, rc)
        o = x_ref[...] + _rms(acc_ref[1 - slot, pl.ds(r0, rc), :], g2_ref[...])
        o_ref[...] = o
        if emit_next:
            hn_ref[...] = _rms(o, gn_ref[...]).astype(BF16)

    @pl.when(i < n)
    def _():
        h = h_ref[...]
        gate = jnp.dot(h, wg_ref[...], preferred_element_type=F32)
        up = jnp.dot(h, wu_ref[...], preferred_element_type=F32)
        a = (_silu(gate) * up).astype(BF16)
        for c0 in range(0, d, dchunk):
            prev = jnp.where(f == 0, 0.0, acc_ref[slot, :, c0:c0 + dchunk])
            acc_ref[slot, :, c0:c0 + dchunk] = prev + jnp.dot(a, wd_ref[:, c0:c0 + dchunk],
                                                                 preferred_element_type=F32)
        epilogue()

    @pl.when(i == n)
    def _():
        epilogue()


def ffn(h, x, wg, wu, wd, g2, g_next, tm=1024, tf=512, dchunk=512, nchunk=8):
    t, d = x.shape
    fh = wg.shape[1]
    tm = _pick_tile(t, tm)
    tf = _pick_tile(fh, tf)
    nf = fh // tf
    n = t // tm
    dchunk = _pick_tile(d, dchunk)
    while nchunk > nf:
        nchunk //= 2
    rc = tm // nchunk
    emit_next = g_next is not None
    chunk = pl.BlockSpec(
        (rc, d), lambda i, f: (jnp.where(i == 0, 0, (i - 1) * nchunk + jnp.minimum(f, nchunk - 1)), 0))
    vec = pl.BlockSpec((1, d), lambda i, f: (0, 0))
    in_specs = [pl.BlockSpec((tm, d), lambda i, f: (jnp.minimum(i, n - 1), 0)), chunk,
                pl.BlockSpec((d, tf), lambda i, f: (0, f)),
                pl.BlockSpec((d, tf), lambda i, f: (0, f)),
                pl.BlockSpec((tf, d), lambda i, f: (f, 0)), vec]
    args = [h, x, wg, wu, wd, g2.reshape(1, d)]
    out_specs, out_shape = [chunk], [jax.ShapeDtypeStruct((t, d), F32)]
    if emit_next:
        in_specs.append(vec)
        args.append(g_next.reshape(1, d))
        out_specs.append(chunk)
        out_shape.append(jax.ShapeDtypeStruct((t, d), BF16))
    outs = pl.pallas_call(
        functools.partial(_ffn_kernel, n=n, nf=nf, nchunk=nchunk, dchunk=dchunk, emit_next=emit_next),
        grid=(n + 1, nf),
        in_specs=in_specs,
        out_specs=out_specs,
        out_shape=out_shape,
        scratch_shapes=[pltpu.VMEM((2, tm, d), F32)],
        compiler_params=_cparams(("arbitrary", "arbitrary")),
        name="ffn",
    )(*args)
    return (outs[0], outs[1]) if emit_next else (outs[0], None)


def _dwconv(ext_ref, zs_ref, w_ref, out_ref, width, hb, tt, d, cb):
    for c0 in range(0, d, cb):
        _dwconv_block(ext_ref, zs_ref, w_ref, None, out_ref, width, hb, tt, c0, cb)


ROW_CHUNK = 64
COL_CHUNK = 256


def _dwconv_block(ext_ref, zs_ref, w_ref, bias_ref, out_ref, width, hb, tt, c0, cb):
    base = hb - (width - 1)
    taps = {}
    for k in range(width):
        taps.setdefault((base + k) % SUBLANES, []).append(k)
    rows = tt + SUBLANES
    cch = min(COL_CHUNK, cb)
    for cc in range(0, cb, cch):
        for r in range(0, rows, ROW_CHUNK):
            nr = min(ROW_CHUNK, rows - r)
            for s, ks in taps.items():
                acc = None
                for k in ks:
                    r0 = base + k - s + r
                    term = ext_ref[r0:r0 + nr, c0 + cc:c0 + cc + cch] * w_ref[k:k + 1, c0 + cc:c0 + cc + cch]
                    acc = term if acc is None else acc + term
                zs_ref[s, r:r + nr, cc:cc + cch] = acc
    for cc in range(0, cb, cch):
        for r in range(0, tt, ROW_CHUNK):
            nr = min(ROW_CHUNK, tt - r)
            out = None
            for s in taps:
                part = zs_ref[s, s + r:s + r + nr, cc:cc + cch]
                out = part if out is None else out + part
            if bias_ref is not None:
                out = out + bias_ref[:, c0 + cc:c0 + cc + cch]
            out_ref[r:r + nr, c0 + cc:c0 + cc + cch] = out


def _carry_ext(ext_ref, cache_ref, hb, tt):
    @pl.when(pl.program_id(1) == 0)
    def _():
        ext_ref[0:hb, :] = cache_ref[0]
        ext_ref[hb + tt:hb + tt + SUBLANES, :] = jnp.zeros((SUBLANES, ext_ref.shape[1]), F32)

    @pl.when(pl.program_id(1) > 0)
    def _():
        ext_ref[0:hb, :] = ext_ref[tt:tt + hb, :]


def _conformer_kernel(x_ref, gpre_ref, w1_ref, b1_ref, cache_ref, dw_ref, dwb_ref, lng_ref, lnb_ref, w2_ref,
                      b2_ref, gpost_ref, gn_ref, o_ref, tail_ref, hn_ref, ext_ref, zs_ref, h_ref, c_ref, y_ref,
                      *, width, hb, tt, d, cb):
    _carry_ext(ext_ref, cache_ref, hb, tt)
    rch = min(ROW_CHUNK, tt)
    for r in range(0, tt, rch):
        h_ref[r:r + rch, :] = _rms(x_ref[0, r:r + rch, :], gpre_ref[...]).astype(BF16)
    h = h_ref[...]
    for c0 in range(0, d, cb):
        a = jnp.dot(h, w1_ref[:, c0:c0 + cb], preferred_element_type=F32) + b1_ref[:, c0:c0 + cb]
        g = jnp.dot(h, w1_ref[:, d + c0:d + c0 + cb], preferred_element_type=F32) + b1_ref[:, d + c0:d + c0 + cb]
        ext_ref[hb:hb + tt, c0:c0 + cb] = a * jax.nn.sigmoid(g)
        _dwconv_block(ext_ref, zs_ref, dw_ref, dwb_ref, c_ref, width, hb, tt, c0, cb)
    for r in range(0, tt, rch):
        c = c_ref[r:r + rch, :]
        cc = c - jnp.mean(c, axis=-1, keepdims=True)
        y = cc * lax.rsqrt(jnp.mean(cc * cc, axis=-1, keepdims=True) + EPS) * lng_ref[...] + lnb_ref[...]
        y_ref[r:r + rch, :] = _silu(y).astype(BF16)
    out = jnp.dot(y_ref[...], w2_ref[...], preferred_element_type=F32)
    for r in range(0, tt, rch):
        o = x_ref[0, r:r + rch, :] + _rms(out[r:r + rch] + b2_ref[...], gpost_ref[...])
        o_ref[0, r:r + rch, :] = o
        hn_ref[0, r:r + rch, :] = _rms(o, gn_ref[...]).astype(BF16)
    tail_ref[0] = ext_ref[tt:tt + hb, :]


def _const_spec(shape, single=False):
    nd = len(shape)
    if single:
        return pl.BlockSpec(shape, lambda b, i: (0,) * nd, pipeline_mode=pl.Buffered(1))
    return pl.BlockSpec(shape, lambda b, i: (0,) * nd)


def conformer_mixer(x, gpre, w1, b1, cache, dw, dwb, lng, lnb, w2, b2, gpost, g_next, tt=256, cb=512):
    bsz, t, d = x.shape
    width = dw.shape[0]
    hb = 32
    tt = _pick_tile(t, tt)
    cb = _pick_tile(d, cb)
    row = lambda a: a.reshape(1, -1)
    xspec = pl.BlockSpec((1, tt, d), lambda b, i: (b, i, 0))
    out, tail, hn = pl.pallas_call(
        functools.partial(_conformer_kernel, width=width, hb=hb, tt=tt, d=d, cb=cb),
        grid=(bsz, t // tt),
        in_specs=[xspec, _const_spec((1, d)), _const_spec((d, 2 * d), single=True), _const_spec((1, 2 * d)),
                  pl.BlockSpec((1, hb, d), lambda b, i: (b, 0, 0)), _const_spec((hb, d)), _const_spec((1, d)),
                  _const_spec((1, d)), _const_spec((1, d)), _const_spec((d, d), single=True),
                  _const_spec((1, d)), _const_spec((1, d)), _const_spec((1, d))],
        out_specs=[xspec, pl.BlockSpec((1, hb, d), lambda b, i: (b, 0, 0)), xspec],
        out_shape=[jax.ShapeDtypeStruct((bsz, t, d), F32), jax.ShapeDtypeStruct((bsz, hb, d), F32),
                   jax.ShapeDtypeStruct((bsz, t, d), BF16)],
        scratch_shapes=[pltpu.VMEM((hb + tt + SUBLANES, d), F32), pltpu.VMEM((SUBLANES, tt + SUBLANES, cb), F32),
                        pltpu.VMEM((tt, d), BF16), pltpu.VMEM((tt, d), F32), pltpu.VMEM((tt, d), BF16)],
        compiler_params=_cparams(("parallel", "arbitrary")),
        name="conformer_mixer",
    )(x, row(gpre), w1, row(b1), _pad_rows(cache, hb), jnp.pad(dw, ((0, hb - width), (0, 0))), row(dwb),
      row(lng), row(lnb), w2, row(b2), row(gpost), row(g_next))
    return out, tail[:, hb - (width - 1):], hn


def _sconv_tail_kernel(gx_ref, cache_ref, bg_ref, cw_ref, w_ref, gpost_ref, gn_ref, x_ref, o_ref, hn_ref, ext_ref,
                       zs_ref, c_ref, y_ref, *, width, hb, tt, d, cb):
    _carry_ext(ext_ref, cache_ref, hb, tt)
    ext_ref[hb:hb + tt, :] = gx_ref[0]
    _dwconv(ext_ref, zs_ref, cw_ref, c_ref, width, hb, tt, d, cb)
    rch = min(ROW_CHUNK, tt)
    for r in range(0, tt, rch):
        y_ref[r:r + rch, :] = (bg_ref[0, r:r + rch, :] * c_ref[r:r + rch, :]).astype(BF16)
    out = jnp.dot(y_ref[...], w_ref[...], preferred_element_type=F32)
    for r in range(0, tt, rch):
        o = x_ref[0, r:r + rch, :] + _rms(out[r:r + rch], gpost_ref[...])
        o_ref[0, r:r + rch, :] = o
        hn_ref[0, r:r + rch, :] = _rms(o, gn_ref[...]).astype(BF16)


def sconv_tail(gx, cache, bg, cw, w_out, gpost, g_next, x, tt=256, cb=512):
    bsz, t, d = gx.shape
    width = cw.shape[0]
    hb = SUBLANES
    tt = _pick_tile(t, tt)
    cb = _pick_tile(d, cb)
    cur = pl.BlockSpec((1, tt, d), lambda b, i: (b, i, 0))
    return pl.pallas_call(
        functools.partial(_sconv_tail_kernel, width=width, hb=hb, tt=tt, d=d, cb=cb),
        grid=(bsz, t // tt),
        in_specs=[cur, pl.BlockSpec((1, hb, d), lambda b, i: (b, 0, 0)), cur, _const_spec((hb, d)),
                  _const_spec((d, d), single=True), _const_spec((1, d)), _const_spec((1, d)), cur],
        out_specs=[cur, cur],
        out_shape=[jax.ShapeDtypeStruct((bsz, t, d), F32), jax.ShapeDtypeStruct((bsz, t, d), BF16)],
        scratch_shapes=[pltpu.VMEM((hb + tt + SUBLANES, d), F32), pltpu.VMEM((SUBLANES, tt + SUBLANES, cb), F32),
                        pltpu.VMEM((tt, d), F32), pltpu.VMEM((tt, d), BF16)],
        compiler_params=_cparams(("parallel", "arbitrary")),
        name="sconv_tail",
    )(gx, _pad_rows(cache, hb), bg, jnp.pad(cw, ((0, hb - width), (0, 0))), w_out, gpost.reshape(1, d),
      g_next.reshape(1, d), x)


def _delta_kernel(*refs, hp, L, has_state):
    hd = DN_HEAD_DIM
    if has_state:
        (q_ref, k_ref, v_ref, z_ref, ba_ref, cq_ref, ck_ref, cv_ref, wq_ref, wk_ref, wv_ref, par_ref, ng_ref,
         s0_ref, o_ref, s_ref, eq_ref, ek_ref, ev_ref) = refs
    else:
        (q_ref, k_ref, v_ref, z_ref, ba_ref, cq_ref, ck_ref, cv_ref, wq_ref, wk_ref, wv_ref, par_ref, ng_ref,
         o_ref, s_ref, eq_ref, ek_ref, ev_ref) = refs
        s0_ref = None
    hb = SUBLANES
    width = 4
    L2 = 2 * L

    @pl.when(pl.program_id(2) == 0)
    def _():
        eq_ref[0:hb, :] = cq_ref[0]
        ek_ref[0:hb, :] = ck_ref[0]
        ev_ref[0:hb, :] = cv_ref[0]
        if has_state:
            s_ref[...] = s0_ref[...]
        else:
            s_ref[...] = jnp.zeros(s_ref.shape, F32)

    eq_ref[hb:hb + L, :] = q_ref[0]
    ek_ref[hb:hb + L, :] = k_ref[0]
    ev_ref[hb:hb + L, :] = v_ref[0]

    def conv_silu(e_ref, w_ref, lo):
        acc = None
        for kk in range(width):
            r0 = hb - (width - 1) + kk
            term = e_ref[r0:r0 + L, lo:lo + hd] * w_ref[kk:kk + 1, lo:lo + hd]
            acc = term if acc is None else acc + term
        return _silu(acc)

    ba = ba_ref[0]
    beta_all = jax.nn.sigmoid(ba)
    g_all = -jnp.exp(par_ref[0, 0:1, :]) * jax.nn.softplus(ba + par_ref[0, 1:2, :])
    ri = lax.broadcasted_iota(jnp.int32, (L, L), 0)
    ci = lax.broadcasted_iota(jnp.int32, (L, L), 1)
    gc_all = _mm_f32(jnp.where(ri >= ci, 1.0, 0.0).astype(F32), g_all)
    gc_t = gc_all.T
    grow_all = jnp.concatenate([gc_t[64:64 + hp], gc_t[96:96 + hp]], axis=1)
    lane_h = lax.broadcasted_iota(jnp.int32, (hp, L2), 1)
    gl_e = jnp.broadcast_to(grow_all[:, L - 1:L], (hp, L2))
    gl_o = jnp.broadcast_to(grow_all[:, L2 - 1:L2], (hp, L2))
    kdec_all = jnp.exp(jnp.where(lane_h < L, gl_e, gl_o) - grow_all)
    egl_e = jnp.exp(gl_e)
    egl_o = jnp.exp(gl_o)

    r2 = lax.broadcasted_iota(jnp.int32, (L, L2), 0)
    l2 = lax.broadcasted_iota(jnp.int32, (L, L2), 1)
    c2 = jnp.bitwise_and(l2, L - 1)
    incl2 = r2 >= c2
    strict2 = r2 > c2
    first2 = l2 < L
    eye2 = jnp.where(r2 == c2, 1.0, 0.0).astype(F32)
    rb = lax.broadcasted_iota(jnp.int32, (L2, L2), 0)
    lb = lax.broadcasted_iota(jnp.int32, (L2, L2), 1)
    bmask = (rb < L) == (lb < L)
    zero_sq = jnp.zeros((L, hd), F32)

    def bdiag(x):
        xb = x.astype(BF16)
        return jnp.where(bmask, jnp.concatenate([xb, xb], axis=0), jnp.zeros((), BF16))

    def bcast(col):
        return jnp.broadcast_to(col, (L, hd))

    pairs = range(hp)
    qn, kn, knt2 = [], [], []
    for j in pairs:
        qj = conv_silu(eq_ref, wq_ref, j * hd)
        kj = conv_silu(ek_ref, wk_ref, j * hd)
        qn.append(qj * lax.rsqrt(jnp.sum(qj * qj, axis=-1, keepdims=True) + EPS) * (hd ** -0.5))
        kn.append(kj * lax.rsqrt(jnp.sum(kj * kj, axis=-1, keepdims=True) + EPS))
        knt2.append(jnp.concatenate([kn[j], kn[j]], axis=0).T)

    a2 = [_mm(jnp.concatenate([qn[j], kn[j]], axis=0), knt2[j]) for j in pairs]

    bcs, egcs, tmat, pmat, qkm2 = [], [], [], [], []
    for j in pairs:
        gce, gco = bcast(gc_all[:, 64 + j:65 + j]), bcast(gc_all[:, 96 + j:97 + j])
        bce, bco = bcast(beta_all[:, j:j + 1]), bcast(beta_all[:, 32 + j:33 + j])
        bcs.append((bce, bco))
        egcs.append((jnp.exp(gce), jnp.exp(gco)))
        gcol2 = jnp.where(first2, gce, gco)
        bcol2 = jnp.where(first2, bce, bco)
        decay2 = jnp.exp(jnp.where(incl2, gcol2 - grow_all[j:j + 1, :], NEG_INF))
        m2 = jnp.where(strict2, bcol2 * a2[j][L:] * decay2, 0.0)
        qkm2.append(jnp.where(incl2, a2[j][:L] * decay2, 0.0))
        tmat.append(eye2 - m2)
        pmat.append(m2)

    nst = int(math.log2(L)) - 1
    pmat = [_mm(pmat[j], bdiag(pmat[j])) for j in pairs]
    for st in range(nst):
        if st < nst - 1:
            outs = [_mm(jnp.concatenate([tmat[j], pmat[j]], axis=0), bdiag(pmat[j])) for j in pairs]
            tmat = [tmat[j] + outs[j][:L] for j in pairs]
            pmat = [outs[j][L:] for j in pairs]
        else:
            tmat = [tmat[j] + _mm(tmat[j], bdiag(pmat[j])) for j in pairs]

    heads = [(j, r) for j in pairs for r in range(2)]
    sols = []
    for j, r in heads:
        h = 2 * j + r
        vh = conv_silu(ev_ref, wv_ref, h * hd)
        bc_, egc_ = bcs[j][r], egcs[j][r]
        rhs = jnp.concatenate([bc_ * vh, (bc_ * egc_) * kn[j]], axis=1).astype(BF16)
        zr = jnp.zeros_like(rhs)
        rhs_pad = jnp.concatenate([rhs, zr] if r == 0 else [zr, rhs], axis=0)
        sols.append(_mm(tmat[j], rhs_pad))

    xs = []
    for idx, (j, r) in enumerate(heads):
        h = 2 * j + r
        w_ = sols[idx][:, hd:]
        xs.append(_mm(jnp.concatenate([w_, qn[j] * egcs[j][r]], axis=0), s_ref[0, h]))

    kgt2 = [knt2[j] * kdec_all[j:j + 1, :] for j in pairs]
    for idx, (j, r) in enumerate(heads):
        h = 2 * j + r
        v_new = sols[idx][:, :hd] - xs[idx][:L]
        vpad = jnp.concatenate([v_new, zero_sq] if r == 0 else [zero_sq, v_new], axis=0)
        y = _mm(jnp.concatenate([qkm2[j], kgt2[j]], axis=0), vpad)
        egl = (egl_e if r == 0 else egl_o)[j:j + 1, :]
        s_ref[0, h] = s_ref[0, h] * egl + y[L:]
        o = xs[idx][L:] + y[:L]
        zh = z_ref[0, :, h * hd:(h + 1) * hd]
        o = o * lax.rsqrt(jnp.mean(o * o, axis=-1, keepdims=True) + EPS) * ng_ref[...] * _silu(zh)
        o_ref[0, :, h * hd:(h + 1) * hd] = o.astype(o_ref.dtype)

    eq_ref[0:hb, :] = eq_ref[L:L + hb, :]
    ek_ref[0:hb, :] = ek_ref[L:L + hb, :]
    ev_ref[0:hb, :] = ev_ref[L:L + hb, :]


def delta_rule(qkv, z, ba, conv_cache, conv_w, par, norm_g, s0, hp):
    bsz, t, _ = qkv.shape
    vdim = z.shape[-1]
    hd = DN_HEAD_DIM
    assert 2 * CHUNK == hd, "head pairs are packed into one lane tile"
    vh = vdim // hd
    qh = vh // 2
    hg = qh // hp
    nh = 2 * hp
    L = CHUNK
    nc = t // L
    hb = SUBLANES
    cache = _pad_rows(conv_cache, hb)
    cw = jnp.pad(conv_w, ((0, hb - conv_w.shape[0]), (0, 0)))
    qw, vw = hp * hd, nh * hd
    koff, voff = qh // hp, 2 * qh // nh
    in_specs = [
        pl.BlockSpec((1, L, qw), lambda b, g, c: (b, c, g)),
        pl.BlockSpec((1, L, qw), lambda b, g, c: (b, c, koff + g)),
        pl.BlockSpec((1, L, vw), lambda b, g, c: (b, c, voff + g)),
        pl.BlockSpec((1, L, vw), lambda b, g, c: (b, c, g)),
        pl.BlockSpec((1, L, LANES), lambda b, g, c: (b, c, g)),
        pl.BlockSpec((1, hb, qw), lambda b, g, c: (b, 0, g)),
        pl.BlockSpec((1, hb, qw), lambda b, g, c: (b, 0, koff + g)),
        pl.BlockSpec((1, hb, vw), lambda b, g, c: (b, 0, voff + g)),
        pl.BlockSpec((hb, qw), lambda b, g, c: (0, g)),
        pl.BlockSpec((hb, qw), lambda b, g, c: (0, koff + g)),
        pl.BlockSpec((hb, vw), lambda b, g, c: (0, voff + g)),
        pl.BlockSpec((1, hb, LANES), lambda b, g, c: (g, 0, 0)),
        pl.BlockSpec((1, hd), lambda b, g, c: (0, 0)),
    ]
    args = [qkv, qkv, qkv, z, ba, cache, cache, cache, cw, cw, cw, par, norm_g.reshape(1, hd)]
    if s0 is not None:
        in_specs.append(pl.BlockSpec((1, nh, hd, hd), lambda b, g, c: (b, g, 0, 0)))
        args.append(s0)
    o, s = pl.pallas_call(
        functools.partial(_delta_kernel, hp=hp, L=L, has_state=s0 is not None),
        grid=(bsz, hg, nc),
        in_specs=in_specs,
        out_specs=[pl.BlockSpec((1, L, vw), lambda b, g, c: (b, c, g)),
                   pl.BlockSpec((1, nh, hd, hd), lambda b, g, c: (b, g, 0, 0))],
        out_shape=[jax.ShapeDtypeStruct((bsz, t, vdim), BF16),
                   jax.ShapeDtypeStruct((bsz, vh, hd, hd), F32)],
        scratch_shapes=[pltpu.VMEM((hb + L, qw), F32), pltpu.VMEM((hb + L, qw), F32),
                        pltpu.VMEM((hb + L, vw), F32)],
        compiler_params=_cparams(("parallel", "parallel", "arbitrary")),
        name="delta_rule",
    )(*args)
    return o, s


def _attn_kernel(q_ref, k0_ref, k1_ref, k2_ref, v0_ref, v1_ref, v2_ref, bias_ref, o_ref, *, nkv, grp, masked):
    hd = SWA_HEAD_DIM
    pw = 2 * hd
    npair = grp // 2
    nq = q_ref.shape[1]
    nk = k0_ref.shape[1] * 3
    nkp = bias_ref.shape[-1]
    c = pl.program_id(1)
    q = q_ref[0]
    zrows = jnp.zeros((nkp - nk, k0_ref.shape[2]), F32)
    k = jnp.concatenate([k0_ref[0], k1_ref[0], k2_ref[0], zrows], axis=0)
    v = jnp.concatenate([v0_ref[0], v1_ref[0], v2_ref[0], zrows], axis=0)
    low = lax.broadcasted_iota(jnp.int32, (nkp, pw), 1) < hd
    if masked:
        col = lax.broadcasted_iota(jnp.int32, (npair * nq, nkp), 1)
        valid = jnp.logical_or(col >= nk, c * CHUNK - WINDOW + col >= 0)
    ones = jnp.ones((nkp, pw), BF16)

    def halves(x, n):
        blk = x[:, (n // 2) * pw:(n // 2 + 1) * pw]
        swp = pltpu.roll(blk, hd, axis=1)
        lo_src, hi_src = (blk, swp) if n % 2 == 0 else (swp, blk)
        return (jnp.where(low, lo_src, 0.0).astype(BF16), jnp.where(low, 0.0, hi_src).astype(BF16))

    heads = range(nkv)
    kh = [halves(k, n) for n in heads]
    vh = [halves(v, n) for n in heads]
    q2 = [jnp.concatenate([q[:, (n * grp + 2 * a) * hd:(n * grp + 2 * a + 2) * hd] for a in range(npair)],
                          axis=0).astype(BF16) for n in heads]
    units = [(n, par) for n in heads for par in range(2)]
    ss = []
    for n, par in units:
        s = _mm_nt(q2[n], kh[n][par]) * (hd ** -0.5) + bias_ref[n, par]
        ss.append(jnp.where(valid, s, NEG_INF) if masked else s)
    ps = [jnp.exp(s - jnp.max(s, axis=-1, keepdims=True)).astype(BF16) for s in ss]
    dens = [jnp.dot(p, ones, preferred_element_type=F32) for p in ps]
    avs = [jnp.dot(ps[i], vh[n][par], preferred_element_type=F32) for i, (n, par) in enumerate(units)]
    for n in heads:
        o2 = avs[2 * n] / dens[2 * n] + avs[2 * n + 1] / dens[2 * n + 1]
        for a in range(npair):
            o_ref[0, :, (n * grp + 2 * a) * hd:(n * grp + 2 * a + 2) * hd] = o2[a * nq:(a + 1) * nq].astype(o_ref.dtype)


def _t5_bucket(rel):
    half = REL_BUCKETS // 2
    max_exact = half // 2
    a = jnp.abs(rel)
    af = jnp.maximum(a, 1).astype(F32)
    large = max_exact + (jnp.log(af / max_exact) / math.log(REL_MAX_DIST / max_exact)
                         * (half - max_exact)).astype(jnp.int32)
    large = jnp.minimum(large, half - 1)
    return jnp.where(rel > 0, half, 0) + jnp.where(a < max_exact, a, large)


def _bias_table(rel_bias, sinks, nkv, grp, n_q, n_k, n_kp):
    npair = grp // 2
    rel = jnp.arange(n_k)[None, :] - WINDOW - jnp.arange(n_q)[:, None]
    bias = jnp.take(rel_bias, _t5_bucket(rel), axis=0).astype(F32)
    bias = jnp.transpose(bias, (2, 0, 1)).reshape(nkv, npair, 2, n_q, n_k)
    bias = jnp.transpose(bias, (0, 2, 1, 3, 4)).reshape(nkv, 2, npair * n_q, n_k)
    sink = jnp.transpose(sinks.astype(F32).reshape(nkv, npair, 2), (0, 2, 1))
    sink = jnp.repeat(sink, n_q, axis=2).reshape(nkv, 2, npair * n_q, 1)
    pad = jnp.full((nkv, 2, npair * n_q, n_kp - n_k - 1), NEG_INF, F32)
    return jnp.concatenate([bias, sink, pad], axis=-1)


def swa_attention(q_src, k_src, v_src, kv_col, rel_bias, sinks, nkv, masked):
    hd = SWA_HEAD_DIM
    nheads = sinks.shape[0]
    grp = nheads // nkv
    assert grp % 2 == 0 and nkv % 2 == 0, "heads are processed in lane-tile pairs"
    qd, kvd = nheads * hd, nkv * hd
    bsz = q_src.shape[0]
    t = k_src.shape[1] if masked else k_src.shape[1] - WINDOW
    nc = t // CHUNK
    nkp = 2 * LANES
    bias = _bias_table(rel_bias, sinks, nkv, grp, CHUNK, WINDOW + CHUNK, nkp)
    if masked:
        rows = [lambda b, c, j=j: jnp.maximum(c + j - 2, 0) for j in range(3)]
    else:
        rows = [lambda b, c, j=j: c + j for j in range(3)]
    kspecs = [pl.BlockSpec((1, CHUNK, kvd), lambda b, c, r=r: (b, r(b, c), kv_col[0])) for r in rows]
    vspecs = [pl.BlockSpec((1, CHUNK, kvd), lambda b, c, r=r: (b, r(b, c), kv_col[1])) for r in rows]
    return pl.pallas_call(
        functools.partial(_attn_kernel, nkv=nkv, grp=grp, masked=masked),
        grid=(bsz, nc),
        in_specs=[pl.BlockSpec((1, CHUNK, qd), lambda b, c: (b, c, 0))] + kspecs + vspecs
        + [pl.BlockSpec((nkv, 2, grp // 2 * CHUNK, nkp), lambda b, c: (0, 0, 0, 0))],
        out_specs=pl.BlockSpec((1, CHUNK, qd), lambda b, c: (b, c, 0)),
        out_shape=jax.ShapeDtypeStruct((bsz, t, qd), BF16),
        compiler_params=_cparams(("parallel", "arbitrary")),
        name="swa_attention",
    )(q_src, k_src, k_src, k_src, v_src, v_src, v_src, bias)


def _delta_heads_per_step(qh):
    return min(16, qh)


def _run_group(x, caches, w, first_chunk):
    conv_a, delta_s, delta_conv, sconv, swa_k, swa_v = caches
    bsz, t, d = x.shape
    depth = w["norm_mix_pre"].shape[0]
    xf = x.reshape(bsz * t, d)
    new = {k: [] for k in ("conv_a", "ds", "dc", "sc", "k", "v")}
    bf = lambda a: a.astype(BF16)
    for i in range(depth):
        mix, j = i % 4, i // 4
        g_pre, g_post, g_ffn = w["norm_mix_pre"][i], w["norm_mix_post"][i], w["norm_ffn_pre"][i]
        if mix == 0:
            width = w["conv_a_dw"].shape[1]
            cache = conv_a[j] if conv_a is not None else jnp.zeros((bsz, width - 1, d), F32)
            x3, tail, hf = conformer_mixer(xf.reshape(bsz, t, d), g_pre, bf(w["conv_a_w1"][j]), w["conv_a_b1"][j],
                                           cache, w["conv_a_dw"][j], w["conv_a_dw_b"][j], w["conv_a_ln_g"][j],
                                           w["conv_a_ln_b"][j], bf(w["conv_a_w2"][j]), w["conv_a_b2"][j], g_post,
                                           g_ffn)
            xf, hf = x3.reshape(bsz * t, d), hf.reshape(bsz * t, d)
            new["conv_a"].append(tail)
        elif mix == 1:
            w_in = w["delta_w_in"][j]
            vh = w["delta_a_log"].shape[1]
            vdim = vh * DN_HEAD_DIM
            qkvd = w["delta_conv_w"].shape[2]
            qh = (qkvd - vdim) // (2 * DN_HEAD_DIM)
            hp = _delta_heads_per_step(qh)
            hg = qh // hp
            (qkv,) = matmul_cols(hm, [bf(w_in[:, :qkvd])], _comb_id, [F32])
            (z,) = matmul_cols(hm, [bf(w_in[:, qkvd:qkvd + vdim])], _comb_id, [F32])

            def lanes4(be, bo, ae, ao):
                pad = [(0, 0)] * (be.ndim - 1) + [(0, 32 - hp)]
                return jnp.concatenate([jnp.pad(a, pad) for a in (be, bo, ae, ao)], axis=-1)

            wb = w_in[:, qkvd + vdim:qkvd + vdim + vh].reshape(d, hg, hp, 2)
            wa = w_in[:, qkvd + vdim + vh:].reshape(d, hg, hp, 2)
            w_ba = lanes4(wb[..., 0], wb[..., 1], wa[..., 0], wa[..., 1]).reshape(d, hg * LANES)
            (ba,) = matmul_cols(hm, [bf(w_ba)], _comb_id, [F32], tn=LANES)
            zl = jnp.zeros((hg, hp), F32)
            alog = w["delta_a_log"][j].reshape(hg, hp, 2)
            dtb = w["delta_dt_bias"][j].reshape(hg, hp, 2)
            par = jnp.stack([lanes4(zl, zl, alog[..., 0], alog[..., 1]),
                             lanes4(zl, zl, dtb[..., 0], dtb[..., 1])], axis=1)
            par = jnp.pad(par, ((0, 0), (0, SUBLANES - 2), (0, 0)))
            cw = w["delta_conv_w"][j]
            cache = delta_conv[j] if delta_conv is not None else jnp.zeros((bsz, cw.shape[0] - 1, qkvd), F32)
            qkv3 = qkv.reshape(bsz, t, qkvd)
            o, s_new = delta_rule(qkv3, z.reshape(bsz, t, vdim), ba.reshape(bsz, t, hg * LANES), cache, cw, par,
                                  w["delta_norm_g"][j], delta_s[j] if delta_s is not None else None, hp)
            xf, hf = matmul_norm_residual(o.reshape(bsz * t, vdim), bf(w["delta_w_out"][j]), g_post, xf, g_ffn)
            new["ds"].append(s_new)
            new["dc"].append(qkv3[:, -(cw.shape[0] - 1):])
        elif mix == 2:
            w_in = bf(w["sconv_w_in"][j])
            bg, gx = matmul_cols(hm, [w_in[:, :d], w_in[:, d:2 * d], w_in[:, 2 * d:]], _comb_sconv, [F32, F32],
                                 tm=1024, tn=512)
            gx = gx.reshape(bsz, t, d)
            cw = w["sconv_w"][j]
            cache = sconv[j] if sconv is not None else jnp.zeros((bsz, cw.shape[0] - 1, d), F32)
            x3, hf = sconv_tail(gx, cache, bg.reshape(bsz, t, d), cw, bf(w["sconv_w_out"][j]), g_post, g_ffn,
                                xf.reshape(bsz, t, d))
            xf, hf = x3.reshape(bsz * t, d), hf.reshape(bsz * t, d)
            new["sc"].append(gx[:, -(cw.shape[0] - 1):])
        else:
            w_qkv = w["swa_w_qkv"][j]
            nheads = w["swa_sinks"].shape[1]
            qd = nheads * SWA_HEAD_DIM
            kvd = (w_qkv.shape[1] - qd) // 2
            nkv = kvd // SWA_HEAD_DIM
            (qkv,) = matmul_cols(hm, [bf(w_qkv)], _comb_id, [F32])
            qkv = qkv.reshape(bsz, t, qd + 2 * kvd)
            if first_chunk:
                o = swa_attention(qkv, qkv, qkv, (qd // kvd, qd // kvd + 1), w["rel_bias"], w["swa_sinks"][j],
                                  nkv, True)
                k_ext, v_ext = qkv[:, :, qd:qd + kvd], qkv[:, :, qd + kvd:]
            else:
                k_ext = jnp.concatenate([swa_k[j].reshape(bsz, WINDOW, kvd), qkv[:, :, qd:qd + kvd]], axis=1)
                v_ext = jnp.concatenate([swa_v[j].reshape(bsz, WINDOW, kvd), qkv[:, :, qd + kvd:]], axis=1)
                o = swa_attention(qkv, k_ext, v_ext, (0, 0), w["rel_bias"], w["swa_sinks"][j], nkv, False)
            xf, hf = matmul_norm_residual(o.reshape(bsz * t, qd), bf(w["swa_w_out"][j]), g_post, xf, g_ffn)
            new["k"].append(k_ext[:, -WINDOW:].reshape(bsz, WINDOW, nkv, SWA_HEAD_DIM))
            new["v"].append(v_ext[:, -WINDOW:].reshape(bsz, WINDOW, nkv, SWA_HEAD_DIM))
        g_next = w["norm_mix_pre"][i + 1] if i + 1 < depth else None
        xf, hm = ffn(hf, xf, bf(w["ffn_w_gate"][i]), bf(w["ffn_w_up"][i]), bf(w["ffn_w_down"][i]),
                     w["norm_ffn_post"][i], g_next)
    return xf.reshape(bsz, t, d), tuple(jnp.stack(new[k]) for k in ("conv_a", "ds", "dc", "sc", "k", "v"))


def kernel(x_prompt, x_sample, cache_conv_a, state_delta_s, state_delta_conv, cache_sconv, cache_swa_k, cache_swa_v, rel_bias, norm_mix_pre, norm_mix_post, norm_ffn_pre, norm_ffn_post, ffn_w_gate, ffn_w_up, ffn_w_down, conv_a_w1, conv_a_b1, conv_a_dw, conv_a_dw_b, conv_a_ln_g, conv_a_ln_b, conv_a_w2, conv_a_b2, delta_w_in, delta_conv_w, delta_a_log, delta_dt_bias, delta_norm_g, delta_w_out, sconv_w_in, sconv_w, sconv_w_out, swa_w_qkv, swa_sinks, swa_w_out):
    w = {
        "rel_bias": rel_bias, "norm_mix_pre": norm_mix_pre, "norm_mix_post": norm_mix_post,
        "norm_ffn_pre": norm_ffn_pre, "norm_ffn_post": norm_ffn_post, "ffn_w_gate": ffn_w_gate,
        "ffn_w_up": ffn_w_up, "ffn_w_down": ffn_w_down, "conv_a_w1": conv_a_w1, "conv_a_b1": conv_a_b1,
        "conv_a_dw": conv_a_dw, "conv_a_dw_b": conv_a_dw_b, "conv_a_ln_g": conv_a_ln_g,
        "conv_a_ln_b": conv_a_ln_b, "conv_a_w2": conv_a_w2, "conv_a_b2": conv_a_b2,
        "delta_w_in": delta_w_in, "delta_conv_w": delta_conv_w, "delta_a_log": delta_a_log,
        "delta_dt_bias": delta_dt_bias, "delta_norm_g": delta_norm_g, "delta_w_out": delta_w_out,
        "sconv_w_in": sconv_w_in, "sconv_w": sconv_w, "sconv_w_out": sconv_w_out,
        "swa_w_qkv": swa_w_qkv, "swa_sinks": swa_sinks, "swa_w_out": swa_w_out,
    }
    y_p, (ca_p, ds_p, dc_p, sc_p, k_p, v_p) = _run_group(x_prompt, (None,) * 6, w, True)
    y_s, (ca_s, ds_s, dc_s, sc_s, k_s, v_s) = _run_group(
        x_sample, (cache_conv_a, state_delta_s, state_delta_conv, cache_sconv, cache_swa_k, cache_swa_v), w, False)
    return (y_p, y_s, ca_p, ca_s, ds_p, ds_s, dc_p, dc_s, sc_p, sc_s, k_p, k_s, v_p, v_s)
```

```python
import functools
import math

import jax
import jax.numpy as jnp
from jax import lax
from jax.experimental import pallas as pl
from jax.experimental.pallas import tpu as pltpu

F32 = jnp.float32
BF16 = jnp.bfloat16
EPS = 1e-6
CHUNK = 64
WINDOW = 128
SWA_HEAD_DIM = 64
DN_HEAD_DIM = 128
REL_BUCKETS = 32
REL_MAX_DIST = 128
V7X_VMEM_BUDGET = 56 * 1024 * 1024
SUBLANES = 8
LANES = 128
NEG_INF = float("-inf")


def _cparams(sem, vmem=V7X_VMEM_BUDGET):
    return pltpu.CompilerParams(dimension_semantics=sem, vmem_limit_bytes=vmem)


def _rms(x, g):
    return x * lax.rsqrt(jnp.mean(x * x, axis=-1, keepdims=True) + EPS) * g


def _silu(x):
    return x * jax.nn.sigmoid(x)


def _mm(a, b):
    return jnp.dot(a.astype(BF16), b.astype(BF16), preferred_element_type=F32)


def _mm_nt(a, b):
    return lax.dot_general(a.astype(BF16), b.astype(BF16), (((1,), (1,)), ((), ())),
                           preferred_element_type=F32)


def _mm_f32(a, b):
    return jnp.dot(a, b, preferred_element_type=F32, precision=lax.Precision.HIGHEST)


def _pick_tile(n, pref):
    t = min(n, pref)
    while n % t:
        t //= 2
    return t


def _pad_rows(a, rows):
    pad = rows - a.shape[-2]
    cfg = [(0, 0)] * a.ndim
    cfg[-2] = (pad, 0)
    return jnp.pad(a, cfg)


def _mm_cols_kernel(*refs, n_w, combine, n_out):
    h = refs[0][...]
    ds = [jnp.dot(h, refs[1 + i][...], preferred_element_type=F32) for i in range(n_w)]
    for o_ref, o in zip(refs[1 + n_w:1 + n_w + n_out], combine(*ds)):
        o_ref[...] = o.astype(o_ref.dtype)


def matmul_cols(h, ws, n, combine, out_dtypes, tm=2048, tn=1024):
    t, k = h.shape
    tm = _pick_tile(t, tm)
    tn = _pick_tile(n, tn)
    assert all(c0 % tn == 0 for _, _, c0 in ws)
    w_specs = [pl.BlockSpec((None, k, tn), lambda i, j, layer=layer, cb=c0 // tn: (layer, 0, cb + j))
               for _, layer, c0 in ws]
    return pl.pallas_call(
        functools.partial(_mm_cols_kernel, n_w=len(ws), combine=combine, n_out=len(out_dtypes)),
        grid=(t // tm, n // tn),
        in_specs=[pl.BlockSpec((tm, k), lambda i, j: (i, 0))] + w_specs,
        out_specs=[pl.BlockSpec((tm, tn), lambda i, j: (i, j)) for _ in out_dtypes],
        out_shape=[jax.ShapeDtypeStruct((t, n), dt) for dt in out_dtypes],
        compiler_params=_cparams(("parallel", "arbitrary")),
        name="matmul_cols",
    )(h, *[a for a, _, _ in ws])


def _comb_id(d):
    return (d,)


def _comb_sconv(bg, cg, xin):
    return (bg, cg * xin)


def _mm_norm_res_kernel(a_ref, w_ref, g_ref, gn_ref, x_ref, o_ref, hn_ref):
    d = jnp.dot(a_ref[...], w_ref[...], preferred_element_type=F32)
    o = x_ref[...] + _rms(d, g_ref[...])
    o_ref[...] = o
    hn_ref[...] = _rms(o, gn_ref[...]).astype(BF16)


def matmul_norm_residual(a, w, layer, g, x, g_next, tm=512):
    t, k = a.shape
    d = w.shape[2]
    tm = _pick_tile(t, tm)
    row = pl.BlockSpec((tm, d), lambda i: (i, 0))
    return pl.pallas_call(
        _mm_norm_res_kernel,
        grid=(t // tm,),
        in_specs=[pl.BlockSpec((tm, k), lambda i: (i, 0)),
                  pl.BlockSpec((None, k, d), lambda i: (layer, 0, 0), pipeline_mode=pl.Buffered(1)),
                  pl.BlockSpec((1, d), lambda i: (0, 0)),
                  pl.BlockSpec((1, d), lambda i: (0, 0)),
                  row],
        out_specs=[row, row],
        out_shape=[jax.ShapeDtypeStruct((t, d), F32), jax.ShapeDtypeStruct((t, d), BF16)],
        compiler_params=_cparams(("parallel",)),
        name="matmul_norm_residual",
    )(a, w, g.reshape(1, d), g_next.reshape(1, d), x)


def _ffn_kernel(*refs, n, nf, nchunk, dchunk, emit_next):
    if emit_next:
        h_ref, x_ref, wg_ref, wu_ref, wd_ref, g2_ref, gn_ref, o_ref, hn_ref, acc_ref = refs
    else:
        h_ref, x_ref, wg_ref, wu_ref, wd_ref, g2_ref, o_ref, acc_ref = refs
    i = pl.program_id(0)
    f = pl.program_id(1)
    slot = lax.rem(i, 2)
    tm, d = acc_ref.shape[1], acc_ref.shape[2]
    rc = tm // nchunk

    @pl.when(jnp.logical_and(i == 0, f == 0))
    def _():
        acc_ref[...] = jnp.zeros(acc_ref.shape, F32)

    def epilogue():
        r0 = pl.multiple_of(jnp.minimum(f, nchunk - 1) * rc, rc)
        o = x_ref[...] + _rms(acc_ref[1 - slot, pl.ds(r0, rc), :], g2_ref[...])
        o_ref[...] = o
        if emit_next:
            hn_ref[...] = _rms(o, gn_ref[...]).astype(BF16)

    @pl.when(i < n)
    def _():
        h = h_ref[...]
        gate = jnp.dot(h, wg_ref[...], preferred_element_type=F32)
        up = jnp.dot(h, wu_ref[...], preferred_element_type=F32)
        a = (_silu(gate) * up).astype(BF16)
        for c0 in range(0, d, dchunk):
            prev = jnp.where(f == 0, 0.0, acc_ref[slot, :, c0:c0 + dchunk])
            acc_ref[slot, :, c0:c0 + dchunk] = prev + jnp.dot(a, wd_ref[:, c0:c0 + dchunk],
                                                                 preferred_element_type=F32)
        epilogue()

    @pl.when(i == n)
    def _():
        epilogue()


def ffn(h, x, wg, wu, wd, layer, g2, g_next, tm=1024, tf=512, dchunk=512, nchunk=8):
    t, d = x.shape
    fh = wg.shape[2]
    tm = _pick_tile(t, tm)
    tf = _pick_tile(fh, tf)
    nf = fh // tf
    n = t // tm
    dchunk = _pick_tile(d, dchunk)
    while nchunk > nf:
        nchunk //= 2
    rc = tm // nchunk
    emit_next = g_next is not None
    chunk = pl.BlockSpec(
        (rc, d), lambda i, f: (jnp.where(i == 0, 0, (i - 1) * nchunk + jnp.minimum(f, nchunk - 1)), 0))
    vec = pl.BlockSpec((1, d), lambda i, f: (0, 0))
    wcol = lambda i, f: jnp.where(i == n, nf - 1, f)
    in_specs = [pl.BlockSpec((tm, d), lambda i, f: (jnp.minimum(i, n - 1), 0)), chunk,
                pl.BlockSpec((None, d, tf), lambda i, f: (layer, 0, wcol(i, f))),
                pl.BlockSpec((None, d, tf), lambda i, f: (layer, 0, wcol(i, f))),
                pl.BlockSpec((None, tf, d), lambda i, f: (layer, wcol(i, f), 0)), vec]
    args = [h, x, wg, wu, wd, g2.reshape(1, d)]
    out_specs, out_shape = [chunk], [jax.ShapeDtypeStruct((t, d), F32)]
    if emit_next:
        in_specs.append(vec)
        args.append(g_next.reshape(1, d))
        out_specs.append(chunk)
        out_shape.append(jax.ShapeDtypeStruct((t, d), BF16))
    outs = pl.pallas_call(
        functools.partial(_ffn_kernel, n=n, nf=nf, nchunk=nchunk, dchunk=dchunk, emit_next=emit_next),
        grid=(n + 1, nf),
        in_specs=in_specs,
        out_specs=out_specs,
        out_shape=out_shape,
        scratch_shapes=[pltpu.VMEM((2, tm, d), F32)],
        compiler_params=_cparams(("arbitrary", "arbitrary")),
        name="ffn",
    )(*args)
    return (outs[0], outs[1]) if emit_next else (outs[0], None)


def _dwconv(ext_ref, zs_ref, w_ref, out_ref, width, hb, tt, d, cb):
    for c0 in range(0, d, cb):
        _dwconv_block(ext_ref, zs_ref, w_ref, None, out_ref, width, hb, tt, c0, cb)


ROW_CHUNK = 64
COL_CHUNK = 256


def _dwconv_block(ext_ref, zs_ref, w_ref, bias_ref, out_ref, width, hb, tt, c0, cb):
    base = hb - (width - 1)
    taps = {}
    for k in range(width):
        taps.setdefault((base + k) % SUBLANES, []).append(k)
    rows = tt + SUBLANES
    cch = min(COL_CHUNK, cb)
    for cc in range(0, cb, cch):
        for r in range(0, rows, ROW_CHUNK):
            nr = min(ROW_CHUNK, rows - r)
            for s, ks in taps.items():
                acc = None
                for k in ks:
                    r0 = base + k - s + r
                    term = ext_ref[r0:r0 + nr, c0 + cc:c0 + cc + cch] * w_ref[k:k + 1, c0 + cc:c0 + cc + cch]
                    acc = term if acc is None else acc + term
                zs_ref[s, r:r + nr, cc:cc + cch] = acc
    for cc in range(0, cb, cch):
        for r in range(0, tt, ROW_CHUNK):
            nr = min(ROW_CHUNK, tt - r)
            out = None
            for s in taps:
                part = zs_ref[s, s + r:s + r + nr, cc:cc + cch]
                out = part if out is None else out + part
            if bias_ref is not None:
                out = out + bias_ref[:, c0 + cc:c0 + cc + cch]
            out_ref[r:r + nr, c0 + cc:c0 + cc + cch] = out


def _carry_ext(ext_ref, cache_ref, hb, tt):
    @pl.when(pl.program_id(1) == 0)
    def _():
        ext_ref[0:hb, :] = cache_ref[0]
        ext_ref[hb + tt:hb + tt + SUBLANES, :] = jnp.zeros((SUBLANES, ext_ref.shape[1]), F32)

    @pl.when(pl.program_id(1) > 0)
    def _():
        ext_ref[0:hb, :] = ext_ref[tt:tt + hb, :]


def _conformer_kernel(x_ref, gpre_ref, w1_ref, b1_ref, cache_ref, dw_ref, dwb_ref, lng_ref, lnb_ref, w2_ref,
                      b2_ref, gpost_ref, gn_ref, o_ref, tail_ref, hn_ref, ext_ref, zs_ref, h_ref, c_ref, y_ref,
                      *, width, hb, tt, d, cb):
    _carry_ext(ext_ref, cache_ref, hb, tt)
    rch = min(ROW_CHUNK, tt)
    for r in range(0, tt, rch):
        h_ref[r:r + rch, :] = _rms(x_ref[0, r:r + rch, :], gpre_ref[...]).astype(BF16)
    h = h_ref[...]
    for c0 in range(0, d, cb):
        a = jnp.dot(h, w1_ref[:, c0:c0 + cb], preferred_element_type=F32) + b1_ref[:, c0:c0 + cb]
        g = jnp.dot(h, w1_ref[:, d + c0:d + c0 + cb], preferred_element_type=F32) + b1_ref[:, d + c0:d + c0 + cb]
        ext_ref[hb:hb + tt, c0:c0 + cb] = a * jax.nn.sigmoid(g)
        _dwconv_block(ext_ref, zs_ref, dw_ref, dwb_ref, c_ref, width, hb, tt, c0, cb)
    for r in range(0, tt, rch):
        c = c_ref[r:r + rch, :]
        cc = c - jnp.mean(c, axis=-1, keepdims=True)
        y = cc * lax.rsqrt(jnp.mean(cc * cc, axis=-1, keepdims=True) + EPS) * lng_ref[...] + lnb_ref[...]
        y_ref[r:r + rch, :] = _silu(y).astype(BF16)
    out = jnp.dot(y_ref[...], w2_ref[...], preferred_element_type=F32)
    for r in range(0, tt, rch):
        o = x_ref[0, r:r + rch, :] + _rms(out[r:r + rch] + b2_ref[...], gpost_ref[...])
        o_ref[0, r:r + rch, :] = o
        hn_ref[0, r:r + rch, :] = _rms(o, gn_ref[...]).astype(BF16)
    tail_ref[0] = ext_ref[tt:tt + hb, :]


def _const_spec(shape):
    nd = len(shape)
    return pl.BlockSpec(shape, lambda b, i: (0,) * nd)


def _layer_spec(shape, layer):
    return pl.BlockSpec((None,) + shape, lambda b, i: (layer,) + (0,) * len(shape), pipeline_mode=pl.Buffered(1))


def conformer_mixer(x, gpre, w1, b1, cache, dw, dwb, lng, lnb, w2, layer, b2, gpost, g_next, tt=256, cb=512):
    bsz, t, d = x.shape
    width = dw.shape[0]
    hb = 32
    tt = _pick_tile(t, tt)
    cb = _pick_tile(d, cb)
    row = lambda a: a.reshape(1, -1)
    xspec = pl.BlockSpec((1, tt, d), lambda b, i: (b, i, 0))
    out, tail, hn = pl.pallas_call(
        functools.partial(_conformer_kernel, width=width, hb=hb, tt=tt, d=d, cb=cb),
        grid=(bsz, t // tt),
        in_specs=[xspec, _const_spec((1, d)), _layer_spec((d, 2 * d), layer), _const_spec((1, 2 * d)),
                  pl.BlockSpec((1, hb, d), lambda b, i: (b, 0, 0)), _const_spec((hb, d)), _const_spec((1, d)),
                  _const_spec((1, d)), _const_spec((1, d)), _layer_spec((d, d), layer),
                  _const_spec((1, d)), _const_spec((1, d)), _const_spec((1, d))],
        out_specs=[xspec, pl.BlockSpec((1, hb, d), lambda b, i: (b, 0, 0)), xspec],
        out_shape=[jax.ShapeDtypeStruct((bsz, t, d), F32), jax.ShapeDtypeStruct((bsz, hb, d), F32),
                   jax.ShapeDtypeStruct((bsz, t, d), BF16)],
        scratch_shapes=[pltpu.VMEM((hb + tt + SUBLANES, d), F32), pltpu.VMEM((SUBLANES, tt + SUBLANES, cb), F32),
                        pltpu.VMEM((tt, d), BF16), pltpu.VMEM((tt, d), F32), pltpu.VMEM((tt, d), BF16)],
        compiler_params=_cparams(("parallel", "arbitrary")),
        name="conformer_mixer",
    )(x, row(gpre), w1, row(b1), _pad_rows(cache, hb), jnp.pad(dw, ((0, hb - width), (0, 0))), row(dwb),
      row(lng), row(lnb), w2, row(b2), row(gpost), row(g_next))
    return out, tail[:, hb - (width - 1):], hn


def _sconv_tail_kernel(gx_ref, cache_ref, bg_ref, cw_ref, w_ref, gpost_ref, gn_ref, x_ref, o_ref, hn_ref, ext_ref,
                       zs_ref, c_ref, y_ref, *, width, hb, tt, d, cb):
    _carry_ext(ext_ref, cache_ref, hb, tt)
    ext_ref[hb:hb + tt, :] = gx_ref[0]
    _dwconv(ext_ref, zs_ref, cw_ref, c_ref, width, hb, tt, d, cb)
    rch = min(ROW_CHUNK, tt)
    for r in range(0, tt, rch):
        y_ref[r:r + rch, :] = (bg_ref[0, r:r + rch, :] * c_ref[r:r + rch, :]).astype(BF16)
    out = jnp.dot(y_ref[...], w_ref[...], preferred_element_type=F32)
    for r in range(0, tt, rch):
        o = x_ref[0, r:r + rch, :] + _rms(out[r:r + rch], gpost_ref[...])
        o_ref[0, r:r + rch, :] = o
        hn_ref[0, r:r + rch, :] = _rms(o, gn_ref[...]).astype(BF16)


def sconv_tail(gx, cache, bg, cw, w_out, layer, gpost, g_next, x, tt=256, cb=512):
    bsz, t, d = gx.shape
    width = cw.shape[0]
    hb = SUBLANES
    tt = _pick_tile(t, tt)
    cb = _pick_tile(d, cb)
    cur = pl.BlockSpec((1, tt, d), lambda b, i: (b, i, 0))
    return pl.pallas_call(
        functools.partial(_sconv_tail_kernel, width=width, hb=hb, tt=tt, d=d, cb=cb),
        grid=(bsz, t // tt),
        in_specs=[cur, pl.BlockSpec((1, hb, d), lambda b, i: (b, 0, 0)), cur, _const_spec((hb, d)),
                  _layer_spec((d, d), layer), _const_spec((1, d)), _const_spec((1, d)), cur],
        out_specs=[cur, cur],
        out_shape=[jax.ShapeDtypeStruct((bsz, t, d), F32), jax.ShapeDtypeStruct((bsz, t, d), BF16)],
        scratch_shapes=[pltpu.VMEM((hb + tt + SUBLANES, d), F32), pltpu.VMEM((SUBLANES, tt + SUBLANES, cb), F32),
                        pltpu.VMEM((tt, d), F32), pltpu.VMEM((tt, d), BF16)],
        compiler_params=_cparams(("parallel", "arbitrary")),
        name="sconv_tail",
    )(gx, _pad_rows(cache, hb), bg, jnp.pad(cw, ((0, hb - width), (0, 0))), w_out, gpost.reshape(1, d),
      g_next.reshape(1, d), x)


def _delta_kernel(*refs, hp, L, has_state):
    hd = DN_HEAD_DIM
    if has_state:
        (q_ref, k_ref, v_ref, z_ref, ba_ref, cq_ref, ck_ref, cv_ref, wq_ref, wk_ref, wv_ref, par_ref, ng_ref,
         s0_ref, o_ref, s_ref, eq_ref, ek_ref, ev_ref) = refs
    else:
        (q_ref, k_ref, v_ref, z_ref, ba_ref, cq_ref, ck_ref, cv_ref, wq_ref, wk_ref, wv_ref, par_ref, ng_ref,
         o_ref, s_ref, eq_ref, ek_ref, ev_ref) = refs
        s0_ref = None
    hb = SUBLANES
    width = 4
    L2 = 2 * L

    @pl.when(pl.program_id(2) == 0)
    def _():
        eq_ref[0:hb, :] = cq_ref[0]
        ek_ref[0:hb, :] = ck_ref[0]
        ev_ref[0:hb, :] = cv_ref[0]
        if has_state:
            s_ref[...] = s0_ref[...]
        else:
            s_ref[...] = jnp.zeros(s_ref.shape, F32)

    eq_ref[hb:hb + L, :] = q_ref[0]
    ek_ref[hb:hb + L, :] = k_ref[0]
    ev_ref[hb:hb + L, :] = v_ref[0]

    def conv_silu(e_ref, w_ref, lo):
        acc = None
        for kk in range(width):
            r0 = hb - (width - 1) + kk
            term = e_ref[r0:r0 + L, lo:lo + hd] * w_ref[kk:kk + 1, lo:lo + hd]
            acc = term if acc is None else acc + term
        return _silu(acc)

    ba = ba_ref[0]
    beta_all = jax.nn.sigmoid(ba)
    g_all = -jnp.exp(par_ref[0, 0:1, :]) * jax.nn.softplus(ba + par_ref[0, 1:2, :])
    ri = lax.broadcasted_iota(jnp.int32, (L, L), 0)
    ci = lax.broadcasted_iota(jnp.int32, (L, L), 1)
    gc_all = _mm_f32(jnp.where(ri >= ci, 1.0, 0.0).astype(F32), g_all)
    gc_t = gc_all.T
    grow_all = jnp.concatenate([gc_t[64:64 + hp], gc_t[96:96 + hp]], axis=1)
    lane_h = lax.broadcasted_iota(jnp.int32, (hp, L2), 1)
    gl_e = jnp.broadcast_to(grow_all[:, L - 1:L], (hp, L2))
    gl_o = jnp.broadcast_to(grow_all[:, L2 - 1:L2], (hp, L2))
    kdec_all = jnp.exp(jnp.where(lane_h < L, gl_e, gl_o) - grow_all)
    egl_e = jnp.exp(gl_e)
    egl_o = jnp.exp(gl_o)

    r2 = lax.broadcasted_iota(jnp.int32, (L, L2), 0)
    l2 = lax.broadcasted_iota(jnp.int32, (L, L2), 1)
    c2 = jnp.bitwise_and(l2, L - 1)
    incl2 = r2 >= c2
    strict2 = r2 > c2
    first2 = l2 < L
    eye2 = jnp.where(r2 == c2, 1.0, 0.0).astype(F32)
    rb = lax.broadcasted_iota(jnp.int32, (L2, L2), 0)
    lb = lax.broadcasted_iota(jnp.int32, (L2, L2), 1)
    bmask = (rb < L) == (lb < L)
    zero_sq = jnp.zeros((L, hd), F32)

    def bdiag(x):
        xb = x.astype(BF16)
        return jnp.where(bmask, jnp.concatenate([xb, xb], axis=0), jnp.zeros((), BF16))

    def bcast(col):
        return jnp.broadcast_to(col, (L, hd))

    pairs = range(hp)
    qn, kn, knt2 = [], [], []
    for j in pairs:
        qj = conv_silu(eq_ref, wq_ref, j * hd)
        kj = conv_silu(ek_ref, wk_ref, j * hd)
        qn.append(qj * lax.rsqrt(jnp.sum(qj * qj, axis=-1, keepdims=True) + EPS) * (hd ** -0.5))
        kn.append(kj * lax.rsqrt(jnp.sum(kj * kj, axis=-1, keepdims=True) + EPS))
        knt2.append(jnp.concatenate([kn[j], kn[j]], axis=0).T)

    a2 = [_mm(jnp.concatenate([qn[j], kn[j]], axis=0), knt2[j]) for j in pairs]

    bcs, egcs, tmat, pmat, qkm2 = [], [], [], [], []
    for j in pairs:
        gce, gco = bcast(gc_all[:, 64 + j:65 + j]), bcast(gc_all[:, 96 + j:97 + j])
        bce, bco = bcast(beta_all[:, j:j + 1]), bcast(beta_all[:, 32 + j:33 + j])
        bcs.append((bce, bco))
        egcs.append((jnp.exp(gce), jnp.exp(gco)))
        gcol2 = jnp.where(first2, gce, gco)
        bcol2 = jnp.where(first2, bce, bco)
        decay2 = jnp.exp(jnp.where(incl2, gcol2 - grow_all[j:j + 1, :], NEG_INF))
        m2 = jnp.where(strict2, bcol2 * a2[j][L:] * decay2, 0.0)
        qkm2.append(jnp.where(incl2, a2[j][:L] * decay2, 0.0))
        tmat.append(eye2 - m2)
        pmat.append(m2)

    nst = int(math.log2(L)) - 1
    pmat = [_mm(pmat[j], bdiag(pmat[j])) for j in pairs]
    for st in range(nst):
        if st < nst - 1:
            outs = [_mm(jnp.concatenate([tmat[j], pmat[j]], axis=0), bdiag(pmat[j])) for j in pairs]
            tmat = [tmat[j] + outs[j][:L] for j in pairs]
            pmat = [outs[j][L:] for j in pairs]
        else:
            tmat = [tmat[j] + _mm(tmat[j], bdiag(pmat[j])) for j in pairs]

    heads = [(j, r) for j in pairs for r in range(2)]
    sols = []
    for j, r in heads:
        h = 2 * j + r
        vh = conv_silu(ev_ref, wv_ref, h * hd)
        bc_, egc_ = bcs[j][r], egcs[j][r]
        rhs = jnp.concatenate([bc_ * vh, (bc_ * egc_) * kn[j]], axis=1).astype(BF16)
        zr = jnp.zeros_like(rhs)
        rhs_pad = jnp.concatenate([rhs, zr] if r == 0 else [zr, rhs], axis=0)
        sols.append(_mm(tmat[j], rhs_pad))

    xs = []
    for idx, (j, r) in enumerate(heads):
        h = 2 * j + r
        w_ = sols[idx][:, hd:]
        xs.append(_mm(jnp.concatenate([w_, qn[j] * egcs[j][r]], axis=0), s_ref[0, h]))

    kgt2 = [knt2[j] * kdec_all[j:j + 1, :] for j in pairs]
    for idx, (j, r) in enumerate(heads):
        h = 2 * j + r
        v_new = sols[idx][:, :hd] - xs[idx][:L]
        vpad = jnp.concatenate([v_new, zero_sq] if r == 0 else [zero_sq, v_new], axis=0)
        y = _mm(jnp.concatenate([qkm2[j], kgt2[j]], axis=0), vpad)
        egl = (egl_e if r == 0 else egl_o)[j:j + 1, :]
        s_ref[0, h] = s_ref[0, h] * egl + y[L:]
        o = xs[idx][L:] + y[:L]
        zh = z_ref[0, :, h * hd:(h + 1) * hd]
        o = o * lax.rsqrt(jnp.mean(o * o, axis=-1, keepdims=True) + EPS) * ng_ref[...] * _silu(zh)
        o_ref[0, :, h * hd:(h + 1) * hd] = o.astype(o_ref.dtype)

    eq_ref[0:hb, :] = eq_ref[L:L + hb, :]
    ek_ref[0:hb, :] = ek_ref[L:L + hb, :]
    ev_ref[0:hb, :] = ev_ref[L:L + hb, :]


def delta_rule(qkv, z, ba, conv_cache, conv_w, par, norm_g, s0, hp):
    bsz, t, _ = qkv.shape
    vdim = z.shape[-1]
    hd = DN_HEAD_DIM
    assert 2 * CHUNK == hd, "head pairs are packed into one lane tile"
    vh = vdim // hd
    qh = vh // 2
    hg = qh // hp
    nh = 2 * hp
    L = CHUNK
    nc = t // L
    hb = SUBLANES
    cache = _pad_rows(conv_cache, hb)
    cw = jnp.pad(conv_w, ((0, hb - conv_w.shape[0]), (0, 0)))
    qw, vw = hp * hd, nh * hd
    koff, voff = qh // hp, 2 * qh // nh
    in_specs = [
        pl.BlockSpec((1, L, qw), lambda b, g, c: (b, c, g)),
        pl.BlockSpec((1, L, qw), lambda b, g, c: (b, c, koff + g)),
        pl.BlockSpec((1, L, vw), lambda b, g, c: (b, c, voff + g)),
        pl.BlockSpec((1, L, vw), lambda b, g, c: (b, c, g)),
        pl.BlockSpec((1, L, LANES), lambda b, g, c: (b, c, g)),
        pl.BlockSpec((1, hb, qw), lambda b, g, c: (b, 0, g)),
        pl.BlockSpec((1, hb, qw), lambda b, g, c: (b, 0, koff + g)),
        pl.BlockSpec((1, hb, vw), lambda b, g, c: (b, 0, voff + g)),
        pl.BlockSpec((hb, qw), lambda b, g, c: (0, g)),
        pl.BlockSpec((hb, qw), lambda b, g, c: (0, koff + g)),
        pl.BlockSpec((hb, vw), lambda b, g, c: (0, voff + g)),
        pl.BlockSpec((1, hb, LANES), lambda b, g, c: (g, 0, 0)),
        pl.BlockSpec((1, hd), lambda b, g, c: (0, 0)),
    ]
    args = [qkv, qkv, qkv, z, ba, cache, cache, cache, cw, cw, cw, par, norm_g.reshape(1, hd)]
    if s0 is not None:
        in_specs.append(pl.BlockSpec((1, nh, hd, hd), lambda b, g, c: (b, g, 0, 0)))
        args.append(s0)
    o, s = pl.pallas_call(
        functools.partial(_delta_kernel, hp=hp, L=L, has_state=s0 is not None),
        grid=(bsz, hg, nc),
        in_specs=in_specs,
        out_specs=[pl.BlockSpec((1, L, vw), lambda b, g, c: (b, c, g)),
                   pl.BlockSpec((1, nh, hd, hd), lambda b, g, c: (b, g, 0, 0))],
        out_shape=[jax.ShapeDtypeStruct((bsz, t, vdim), BF16),
                   jax.ShapeDtypeStruct((bsz, vh, hd, hd), F32)],
        scratch_shapes=[pltpu.VMEM((hb + L, qw), F32), pltpu.VMEM((hb + L, qw), F32),
                        pltpu.VMEM((hb + L, vw), F32)],
        compiler_params=_cparams(("parallel", "parallel", "arbitrary")),
        name="delta_rule",
    )(*args)
    return o, s


def _attn_kernel(q_ref, k0_ref, k1_ref, k2_ref, v0_ref, v1_ref, v2_ref, bias_ref, o_ref, *, nkv, grp, masked):
    hd = SWA_HEAD_DIM
    pw = 2 * hd
    npair = grp // 2
    nq = q_ref.shape[1]
    nk = k0_ref.shape[1] * 3
    nkp = bias_ref.shape[-1]
    c = pl.program_id(1)
    q = q_ref[0]
    zrows = jnp.zeros((nkp - nk, k0_ref.shape[2]), F32)
    k = jnp.concatenate([k0_ref[0], k1_ref[0], k2_ref[0], zrows], axis=0)
    v = jnp.concatenate([v0_ref[0], v1_ref[0], v2_ref[0], zrows], axis=0)
    low = lax.broadcasted_iota(jnp.int32, (nkp, pw), 1) < hd
    if masked:
        col = lax.broadcasted_iota(jnp.int32, (npair * nq, nkp), 1)
        valid = jnp.logical_or(col >= nk, c * CHUNK - WINDOW + col >= 0)
    ones = jnp.ones((nkp, pw), BF16)

    def halves(x, n):
        blk = x[:, (n // 2) * pw:(n // 2 + 1) * pw]
        swp = pltpu.roll(blk, hd, axis=1)
        lo_src, hi_src = (blk, swp) if n % 2 == 0 else (swp, blk)
        return (jnp.where(low, lo_src, 0.0).astype(BF16), jnp.where(low, 0.0, hi_src).astype(BF16))

    heads = range(nkv)
    kh = [halves(k, n) for n in heads]
    vh = [halves(v, n) for n in heads]
    q2 = [jnp.concatenate([q[:, (n * grp + 2 * a) * hd:(n * grp + 2 * a + 2) * hd] for a in range(npair)],
                          axis=0).astype(BF16) for n in heads]
    units = [(n, par) for n in heads for par in range(2)]
    ss = []
    for n, par in units:
        s = _mm_nt(q2[n], kh[n][par]) * (hd ** -0.5) + bias_ref[n, par]
        ss.append(jnp.where(valid, s, NEG_INF) if masked else s)
    ps = [jnp.exp(s - jnp.max(s, axis=-1, keepdims=True)).astype(BF16) for s in ss]
    dens = [jnp.dot(p, ones, preferred_element_type=F32) for p in ps]
    avs = [jnp.dot(ps[i], vh[n][par], preferred_element_type=F32) for i, (n, par) in enumerate(units)]
    for n in heads:
        o2 = avs[2 * n] / dens[2 * n] + avs[2 * n + 1] / dens[2 * n + 1]
        for a in range(npair):
            o_ref[0, :, (n * grp + 2 * a) * hd:(n * grp + 2 * a + 2) * hd] = o2[a * nq:(a + 1) * nq].astype(o_ref.dtype)


def _t5_bucket(rel):
    half = REL_BUCKETS // 2
    max_exact = half // 2
    a = jnp.abs(rel)
    af = jnp.maximum(a, 1).astype(F32)
    large = max_exact + (jnp.log(af / max_exact) / math.log(REL_MAX_DIST / max_exact)
                         * (half - max_exact)).astype(jnp.int32)
    large = jnp.minimum(large, half - 1)
    return jnp.where(rel > 0, half, 0) + jnp.where(a < max_exact, a, large)


def _bias_table(rel_bias, sinks, nkv, grp, n_q, n_k, n_kp):
    npair = grp // 2
    rel = jnp.arange(n_k)[None, :] - WINDOW - jnp.arange(n_q)[:, None]
    onehot = (_t5_bucket(rel)[..., None] == jnp.arange(rel_bias.shape[0])).astype(F32)
    bias = jnp.einsum("qkb,bh->qkh", onehot, rel_bias.astype(F32), precision=lax.Precision.HIGHEST)
    bias = jnp.transpose(bias, (2, 0, 1)).reshape(nkv, npair, 2, n_q, n_k)
    bias = jnp.transpose(bias, (0, 2, 1, 3, 4)).reshape(nkv, 2, npair * n_q, n_k)
    sink = jnp.transpose(sinks.astype(F32).reshape(nkv, npair, 2), (0, 2, 1))
    sink = jnp.repeat(sink, n_q, axis=2).reshape(nkv, 2, npair * n_q, 1)
    pad = jnp.full((nkv, 2, npair * n_q, n_kp - n_k - 1), NEG_INF, F32)
    return jnp.concatenate([bias, sink, pad], axis=-1)


def swa_attention(q_src, k_src, v_src, kv_col, rel_bias, sinks, nkv, masked):
    hd = SWA_HEAD_DIM
    nheads = sinks.shape[0]
    grp = nheads // nkv
    assert grp % 2 == 0 and nkv % 2 == 0, "heads are processed in lane-tile pairs"
    qd, kvd = nheads * hd, nkv * hd
    bsz = q_src.shape[0]
    t = k_src.shape[1] if masked else k_src.shape[1] - WINDOW
    nc = t // CHUNK
    nkp = 2 * LANES
    bias = _bias_table(rel_bias, sinks, nkv, grp, CHUNK, WINDOW + CHUNK, nkp)
    if masked:
        rows = [lambda b, c, j=j: jnp.maximum(c + j - 2, 0) for j in range(3)]
    else:
        rows = [lambda b, c, j=j: c + j for j in range(3)]
    kspecs = [pl.BlockSpec((1, CHUNK, kvd), lambda b, c, r=r: (b, r(b, c), kv_col[0])) for r in rows]
    vspecs = [pl.BlockSpec((1, CHUNK, kvd), lambda b, c, r=r: (b, r(b, c), kv_col[1])) for r in rows]
    return pl.pallas_call(
        functools.partial(_attn_kernel, nkv=nkv, grp=grp, masked=masked),
        grid=(bsz, nc),
        in_specs=[pl.BlockSpec((1, CHUNK, qd), lambda b, c: (b, c, 0))] + kspecs + vspecs
        + [pl.BlockSpec((nkv, 2, grp // 2 * CHUNK, nkp), lambda b, c: (0, 0, 0, 0))],
        out_specs=pl.BlockSpec((1, CHUNK, qd), lambda b, c: (b, c, 0)),
        out_shape=jax.ShapeDtypeStruct((bsz, t, qd), BF16),
        compiler_params=_cparams(("parallel", "arbitrary")),
        name="swa_attention",
    )(q_src, k_src, k_src, k_src, v_src, v_src, v_src, bias)


def _delta_heads_per_step(qh):
    return min(16, qh)


def _run_group(x, caches, w, wb, first_chunk):
    conv_a, delta_s, delta_conv, sconv, swa_k, swa_v = caches
    bsz, t, d = x.shape
    depth = w["norm_mix_pre"].shape[0]
    xf = x.reshape(bsz * t, d)
    new = {k: [] for k in ("conv_a", "ds", "dc", "sc", "k", "v")}
    for i in range(depth):
        mix, j = i % 4, i // 4
        g_pre, g_post, g_ffn = w["norm_mix_pre"][i], w["norm_mix_post"][i], w["norm_ffn_pre"][i]
        if mix == 0:
            width = w["conv_a_dw"].shape[1]
            cache = conv_a[j] if conv_a is not None else jnp.zeros((bsz, width - 1, d), F32)
            x3, tail, hf = conformer_mixer(xf.reshape(bsz, t, d), g_pre, wb["conv_a_w1"], w["conv_a_b1"][j],
                                           cache, w["conv_a_dw"][j], w["conv_a_dw_b"][j], w["conv_a_ln_g"][j],
                                           w["conv_a_ln_b"][j], wb["conv_a_w2"], j, w["conv_a_b2"][j], g_post,
                                           g_ffn)
            xf, hf = x3.reshape(bsz * t, d), hf.reshape(bsz * t, d)
            new["conv_a"].append(tail)
        elif mix == 1:
            w_in = w["delta_w_in"][j]
            vh = w["delta_a_log"].shape[1]
            vdim = vh * DN_HEAD_DIM
            qkvd = w["delta_conv_w"].shape[2]
            qh = (qkvd - vdim) // (2 * DN_HEAD_DIM)
            hp = _delta_heads_per_step(qh)
            hg = qh // hp
            (qkv,) = matmul_cols(hm, [(wb["delta_w_in"], j, 0)], qkvd, _comb_id, [F32])
            (z,) = matmul_cols(hm, [(wb["delta_w_in"], j, qkvd)], vdim, _comb_id, [F32])

            def lanes4(be, bo, ae, ao):
                pad = [(0, 0)] * (be.ndim - 1) + [(0, 32 - hp)]
                return jnp.concatenate([jnp.pad(a, pad) for a in (be, bo, ae, ao)], axis=-1)

            w_b = w_in[:, qkvd + vdim:qkvd + vdim + vh].reshape(d, hg, hp, 2)
            w_a = w_in[:, qkvd + vdim + vh:].reshape(d, hg, hp, 2)
            w_ba = lanes4(w_b[..., 0], w_b[..., 1], w_a[..., 0], w_a[..., 1]).reshape(d, hg * LANES)
            (ba,) = matmul_cols(hm, [(w_ba.astype(BF16)[None], 0, 0)], hg * LANES, _comb_id, [F32], tn=LANES)
            zl = jnp.zeros((hg, hp), F32)
            alog = w["delta_a_log"][j].reshape(hg, hp, 2)
            dtb = w["delta_dt_bias"][j].reshape(hg, hp, 2)
            par = jnp.stack([lanes4(zl, zl, alog[..., 0], alog[..., 1]),
                             lanes4(zl, zl, dtb[..., 0], dtb[..., 1])], axis=1)
            par = jnp.pad(par, ((0, 0), (0, SUBLANES - 2), (0, 0)))
            cw = w["delta_conv_w"][j]
            cache = delta_conv[j] if delta_conv is not None else jnp.zeros((bsz, cw.shape[0] - 1, qkvd), F32)
            qkv3 = qkv.reshape(bsz, t, qkvd)
            o, s_new = delta_rule(qkv3, z.reshape(bsz, t, vdim), ba.reshape(bsz, t, hg * LANES), cache, cw, par,
                                  w["delta_norm_g"][j], delta_s[j] if delta_s is not None else None, hp)
            xf, hf = matmul_norm_residual(o.reshape(bsz * t, vdim), wb["delta_w_out"], j, g_post, xf, g_ffn)
            new["ds"].append(s_new)
            new["dc"].append(qkv3[:, -(cw.shape[0] - 1):])
        elif mix == 2:
            w_in = wb["sconv_w_in"]
            bg, gx = matmul_cols(hm, [(w_in, j, 0), (w_in, j, d), (w_in, j, 2 * d)], d, _comb_sconv, [F32, F32],
                                 tm=1024, tn=512)
            gx = gx.reshape(bsz, t, d)
            cw = w["sconv_w"][j]
            cache = sconv[j] if sconv is not None else jnp.zeros((bsz, cw.shape[0] - 1, d), F32)
            x3, hf = sconv_tail(gx, cache, bg.reshape(bsz, t, d), cw, wb["sconv_w_out"], j, g_post, g_ffn,
                                xf.reshape(bsz, t, d))
            xf, hf = x3.reshape(bsz * t, d), hf.reshape(bsz * t, d)
            new["sc"].append(gx[:, -(cw.shape[0] - 1):])
        else:
            nheads = w["swa_sinks"].shape[1]
            qd = nheads * SWA_HEAD_DIM
            kvd = (w["swa_w_qkv"].shape[2] - qd) // 2
            nkv = kvd // SWA_HEAD_DIM
            (qkv,) = matmul_cols(hm, [(wb["swa_w_qkv"], j, 0)], qd + 2 * kvd, _comb_id, [F32])
            qkv = qkv.reshape(bsz, t, qd + 2 * kvd)
            if first_chunk:
                o = swa_attention(qkv, qkv, qkv, (qd // kvd, qd // kvd + 1), w["rel_bias"], w["swa_sinks"][j],
                                  nkv, True)
                k_ext, v_ext = qkv[:, :, qd:qd + kvd], qkv[:, :, qd + kvd:]
            else:
                k_ext = jnp.concatenate([swa_k[j].reshape(bsz, WINDOW, kvd), qkv[:, :, qd:qd + kvd]], axis=1)
                v_ext = jnp.concatenate([swa_v[j].reshape(bsz, WINDOW, kvd), qkv[:, :, qd + kvd:]], axis=1)
                o = swa_attention(qkv, k_ext, v_ext, (0, 0), w["rel_bias"], w["swa_sinks"][j], nkv, False)
            xf, hf = matmul_norm_residual(o.reshape(bsz * t, qd), wb["swa_w_out"], j, g_post, xf, g_ffn)
            new["k"].append(k_ext[:, -WINDOW:].reshape(bsz, WINDOW, nkv, SWA_HEAD_DIM))
            new["v"].append(v_ext[:, -WINDOW:].reshape(bsz, WINDOW, nkv, SWA_HEAD_DIM))
        g_next = w["norm_mix_pre"][i + 1] if i + 1 < depth else None
        xf, hm = ffn(hf, xf, wb["ffn_w_gate"], wb["ffn_w_up"], wb["ffn_w_down"], i, w["norm_ffn_post"][i], g_next)
    return xf.reshape(bsz, t, d), tuple(jnp.stack(new[k]) for k in ("conv_a", "ds", "dc", "sc", "k", "v"))


def kernel(x_prompt, x_sample, cache_conv_a, state_delta_s, state_delta_conv, cache_sconv, cache_swa_k, cache_swa_v, rel_bias, norm_mix_pre, norm_mix_post, norm_ffn_pre, norm_ffn_post, ffn_w_gate, ffn_w_up, ffn_w_down, conv_a_w1, conv_a_b1, conv_a_dw, conv_a_dw_b, conv_a_ln_g, conv_a_ln_b, conv_a_w2, conv_a_b2, delta_w_in, delta_conv_w, delta_a_log, delta_dt_bias, delta_norm_g, delta_w_out, sconv_w_in, sconv_w, sconv_w_out, swa_w_qkv, swa_sinks, swa_w_out):
    w = {
        "rel_bias": rel_bias, "norm_mix_pre": norm_mix_pre, "norm_mix_post": norm_mix_post,
        "norm_ffn_pre": norm_ffn_pre, "norm_ffn_post": norm_ffn_post, "ffn_w_gate": ffn_w_gate,
        "ffn_w_up": ffn_w_up, "ffn_w_down": ffn_w_down, "conv_a_w1": conv_a_w1, "conv_a_b1": conv_a_b1,
        "conv_a_dw": conv_a_dw, "conv_a_dw_b": conv_a_dw_b, "conv_a_ln_g": conv_a_ln_g,
        "conv_a_ln_b": conv_a_ln_b, "conv_a_w2": conv_a_w2, "conv_a_b2": conv_a_b2,
        "delta_w_in": delta_w_in, "delta_conv_w": delta_conv_w, "delta_a_log": delta_a_log,
        "delta_dt_bias": delta_dt_bias, "delta_norm_g": delta_norm_g, "delta_w_out": delta_w_out,
        "sconv_w_in": sconv_w_in, "sconv_w": sconv_w, "sconv_w_out": sconv_w_out,
        "swa_w_qkv": swa_w_qkv, "swa_sinks": swa_sinks, "swa_w_out": swa_w_out,
    }
    wb = {name: w[name].astype(BF16) for name in (
        "ffn_w_gate", "ffn_w_up", "ffn_w_down", "conv_a_w1", "conv_a_w2", "delta_w_in", "delta_w_out",
        "sconv_w_in", "sconv_w_out", "swa_w_qkv", "swa_w_out")}
    y_p, (ca_p, ds_p, dc_p, sc_p, k_p, v_p) = _run_group(x_prompt, (None,) * 6, w, wb, True)
    y_s, (ca_s, ds_s, dc_s, sc_s, k_s, v_s) = _run_group(
        x_sample, (cache_conv_a, state_delta_s, state_delta_conv, cache_sconv, cache_swa_k, cache_swa_v), w, wb,
        False)
    return (y_p, y_s, ca_p, ca_s, ds_p, ds_s, dc_p, dc_s, sc_p, sc_s, k_p, k_s, v_p, v_s)
```

```python
import functools
import math

import jax
import jax.numpy as jnp
from jax import lax
from jax.experimental import pallas as pl
from jax.experimental.pallas import tpu as pltpu

F32 = jnp.float32
BF16 = jnp.bfloat16
EPS = 1e-6
CHUNK = 64
WINDOW = 128
SWA_HEAD_DIM = 64
DN_HEAD_DIM = 128
REL_BUCKETS = 32
REL_MAX_DIST = 128
V7X_VMEM_BUDGET = 56 * 1024 * 1024
SUBLANES = 8
LANES = 128
NEG_INF = float("-inf")


def _cparams(sem, vmem=V7X_VMEM_BUDGET):
    return pltpu.CompilerParams(dimension_semantics=sem, vmem_limit_bytes=vmem)


def _rms(x, g):
    return x * lax.rsqrt(jnp.mean(x * x, axis=-1, keepdims=True) + EPS) * g


def _silu(x):
    return x * jax.nn.sigmoid(x)


def _mm(a, b):
    return jnp.dot(a.astype(BF16), b.astype(BF16), preferred_element_type=F32)


def _mm_nt(a, b):
    return lax.dot_general(a.astype(BF16), b.astype(BF16), (((1,), (1,)), ((), ())),
                           preferred_element_type=F32)


def _mm_f32(a, b):
    return jnp.dot(a, b, preferred_element_type=F32, precision=lax.Precision.HIGHEST)


def _pick_tile(n, pref):
    t = min(n, pref)
    while n % t:
        t //= 2
    return t


def _pad_rows(a, rows):
    pad = rows - a.shape[-2]
    cfg = [(0, 0)] * a.ndim
    cfg[-2] = (pad, 0)
    return jnp.pad(a, cfg)


def _mm_cols_kernel(*refs, n_w, combine, n_out):
    h = refs[0][...]
    ds = [jnp.dot(h, refs[1 + i][...], preferred_element_type=F32) for i in range(n_w)]
    for o_ref, o in zip(refs[1 + n_w:1 + n_w + n_out], combine(*ds)):
        o_ref[...] = o.astype(o_ref.dtype)


def matmul_cols(h, ws, n, combine, out_dtypes, tm=2048, tn=1024):
    t, k = h.shape
    tm = _pick_tile(t, tm)
    tn = _pick_tile(n, tn)
    assert all(c0 % tn == 0 for _, _, c0 in ws)
    w_specs = [pl.BlockSpec((None, k, tn), lambda i, j, layer=layer, cb=c0 // tn: (layer, 0, cb + j))
               for _, layer, c0 in ws]
    return pl.pallas_call(
        functools.partial(_mm_cols_kernel, n_w=len(ws), combine=combine, n_out=len(out_dtypes)),
        grid=(t // tm, n // tn),
        in_specs=[pl.BlockSpec((tm, k), lambda i, j: (i, 0))] + w_specs,
        out_specs=[pl.BlockSpec((tm, tn), lambda i, j: (i, j)) for _ in out_dtypes],
        out_shape=[jax.ShapeDtypeStruct((t, n), dt) for dt in out_dtypes],
        compiler_params=_cparams(("parallel", "arbitrary")),
        name="matmul_cols",
    )(h, *[a for a, _, _ in ws])


def _comb_id(d):
    return (d,)


def _comb_sconv(bg, cg, xin):
    return (bg, cg * xin)


def _mm_norm_res_kernel(a_ref, w_ref, g_ref, gn_ref, x_ref, o_ref, hn_ref):
    d = jnp.dot(a_ref[...], w_ref[...], preferred_element_type=F32)
    o = x_ref[...] + _rms(d, g_ref[...])
    o_ref[...] = o
    hn_ref[...] = _rms(o, gn_ref[...]).astype(BF16)


def matmul_norm_residual(a, w, layer, g, x, g_next, tm=512):
    t, k = a.shape
    d = w.shape[2]
    tm = _pick_tile(t, tm)
    row = pl.BlockSpec((tm, d), lambda i: (i, 0))
    return pl.pallas_call(
        _mm_norm_res_kernel,
        grid=(t // tm,),
        in_specs=[pl.BlockSpec((tm, k), lambda i: (i, 0)),
                  pl.BlockSpec((None, k, d), lambda i: (layer, 0, 0), pipeline_mode=pl.Buffered(1)),
                  pl.BlockSpec((1, d), lambda i: (0, 0)),
                  pl.BlockSpec((1, d), lambda i: (0, 0)),
                  row],
        out_specs=[row, row],
        out_shape=[jax.ShapeDtypeStruct((t, d), F32), jax.ShapeDtypeStruct((t, d), BF16)],
        compiler_params=_cparams(("parallel",)),
        name="matmul_norm_residual",
    )(a, w, g.reshape(1, d), g_next.reshape(1, d), x)


def _ffn_kernel(*refs, n, nf, nchunk, dchunk, emit_next):
    if emit_next:
        h_ref, x_ref, wg_ref, wu_ref, wd_ref, g2_ref, gn_ref, o_ref, hn_ref, acc_ref = refs
    else:
        h_ref, x_ref, wg_ref, wu_ref, wd_ref, g2_ref, o_ref, acc_ref = refs
    i = pl.program_id(0)
    f = pl.program_id(1)
    slot = lax.rem(i, 2)
    tm, d = acc_ref.shape[1], acc_ref.shape[2]
    rc = tm // nchunk

    @pl.when(jnp.logical_and(i == 0, f == 0))
    def _():
        acc_ref[...] = jnp.zeros(acc_ref.shape, F32)

    def epilogue():
        r0 = pl.multiple_of(jnp.minimum(f, nchunk - 1) * rc, rc)
        o = x_ref[...] + _rms(acc_ref[1 - slot, pl.ds(r0, rc), :], g2_ref[...])
        o_ref[...] = o
        if emit_next:
            hn_ref[...] = _rms(o, gn_ref[...]).astype(BF16)

    @pl.when(i < n)
    def _():
        epilogue()
        hr = tm // 2
        for r0 in range(0, tm, hr):
            h = h_ref[r0:r0 + hr, :]
            gate = jnp.dot(h, wg_ref[...], preferred_element_type=F32)
            up = jnp.dot(h, wu_ref[...], preferred_element_type=F32)
            a = (_silu(gate) * up).astype(BF16)
            for c0 in range(0, d, dchunk):
                prev = jnp.where(f == 0, 0.0, acc_ref[slot, r0:r0 + hr, c0:c0 + dchunk])
                acc_ref[slot, r0:r0 + hr, c0:c0 + dchunk] = prev + jnp.dot(
                    a, wd_ref[:, c0:c0 + dchunk], preferred_element_type=F32)

    @pl.when(i == n)
    def _():
        epilogue()


def ffn(h, x, wg, wu, wd, layer, g2, g_next, tm=1024, tf=512, dchunk=512, nchunk=8):
    t, d = x.shape
    fh = wg.shape[2]
    tm = _pick_tile(t, tm)
    tf = _pick_tile(fh, tf)
    nf = fh // tf
    n = t // tm
    dchunk = _pick_tile(d, dchunk)
    while nchunk > nf:
        nchunk //= 2
    rc = tm // nchunk
    emit_next = g_next is not None
    chunk = pl.BlockSpec(
        (rc, d), lambda i, f: (jnp.where(i == 0, 0, (i - 1) * nchunk + jnp.minimum(f, nchunk - 1)), 0))
    vec = pl.BlockSpec((1, d), lambda i, f: (0, 0))
    wcol = lambda i, f: jnp.where(i == n, nf - 1, f)
    in_specs = [pl.BlockSpec((tm, d), lambda i, f: (jnp.minimum(i, n - 1), 0)), chunk,
                pl.BlockSpec((None, d, tf), lambda i, f: (layer, 0, wcol(i, f))),
                pl.BlockSpec((None, d, tf), lambda i, f: (layer, 0, wcol(i, f))),
                pl.BlockSpec((None, tf, d), lambda i, f: (layer, wcol(i, f), 0)), vec]
    args = [h, x, wg, wu, wd, g2.reshape(1, d)]
    out_specs, out_shape = [chunk], [jax.ShapeDtypeStruct((t, d), F32)]
    if emit_next:
        in_specs.append(vec)
        args.append(g_next.reshape(1, d))
        out_specs.append(chunk)
        out_shape.append(jax.ShapeDtypeStruct((t, d), BF16))
    outs = pl.pallas_call(
        functools.partial(_ffn_kernel, n=n, nf=nf, nchunk=nchunk, dchunk=dchunk, emit_next=emit_next),
        grid=(n + 1, nf),
        in_specs=in_specs,
        out_specs=out_specs,
        out_shape=out_shape,
        scratch_shapes=[pltpu.VMEM((2, tm, d), F32)],
        compiler_params=_cparams(("arbitrary", "arbitrary")),
        name="ffn",
    )(*args)
    return (outs[0], outs[1]) if emit_next else (outs[0], None)


def _dwconv(ext_ref, zs_ref, w_ref, out_ref, width, hb, tt, d, cb):
    for c0 in range(0, d, cb):
        _dwconv_block(ext_ref, zs_ref, w_ref, None, out_ref, width, hb, tt, c0, cb)


ROW_CHUNK = 64
COL_CHUNK = 256


def _dwconv_block(ext_ref, zs_ref, w_ref, bias_ref, out_ref, width, hb, tt, c0, cb):
    base = hb - (width - 1)
    taps = {}
    for k in range(width):
        taps.setdefault((base + k) % SUBLANES, []).append(k)
    rows = tt + SUBLANES
    cch = min(COL_CHUNK, cb)
    for cc in range(0, cb, cch):
        for r in range(0, rows, ROW_CHUNK):
            nr = min(ROW_CHUNK, rows - r)
            for s, ks in taps.items():
                acc = None
                for k in ks:
                    r0 = base + k - s + r
                    term = ext_ref[r0:r0 + nr, c0 + cc:c0 + cc + cch] * w_ref[k:k + 1, c0 + cc:c0 + cc + cch]
                    acc = term if acc is None else acc + term
                zs_ref[s, r:r + nr, cc:cc + cch] = acc
    for cc in range(0, cb, cch):
        for r in range(0, tt, ROW_CHUNK):
            nr = min(ROW_CHUNK, tt - r)
            out = None
            for s in taps:
                part = zs_ref[s, s + r:s + r + nr, cc:cc + cch]
                out = part if out is None else out + part
            if bias_ref is not None:
                out = out + bias_ref[:, c0 + cc:c0 + cc + cch]
            out_ref[r:r + nr, c0 + cc:c0 + cc + cch] = out


def _carry_ext(ext_ref, cache_ref, hb, tt):
    @pl.when(pl.program_id(1) == 0)
    def _():
        ext_ref[0:hb, :] = cache_ref[0]
        ext_ref[hb + tt:hb + tt + SUBLANES, :] = jnp.zeros((SUBLANES, ext_ref.shape[1]), F32)

    @pl.when(pl.program_id(1) > 0)
    def _():
        ext_ref[0:hb, :] = ext_ref[tt:tt + hb, :]


def _conformer_kernel(x_ref, gpre_ref, w1_ref, b1_ref, cache_ref, dw_ref, dwb_ref, lng_ref, lnb_ref, w2_ref,
                      b2_ref, gpost_ref, gn_ref, o_ref, tail_ref, hn_ref, ext_ref, zs_ref, h_ref, c_ref, y_ref,
                      *, width, hb, tt, d, cb):
    _carry_ext(ext_ref, cache_ref, hb, tt)
    rch = min(ROW_CHUNK, tt)
    for r in range(0, tt, rch):
        h_ref[r:r + rch, :] = _rms(x_ref[0, r:r + rch, :], gpre_ref[...]).astype(BF16)
    h = h_ref[...]
    for c0 in range(0, d, cb):
        a = jnp.dot(h, w1_ref[:, c0:c0 + cb], preferred_element_type=F32) + b1_ref[:, c0:c0 + cb]
        g = jnp.dot(h, w1_ref[:, d + c0:d + c0 + cb], preferred_element_type=F32) + b1_ref[:, d + c0:d + c0 + cb]
        ext_ref[hb:hb + tt, c0:c0 + cb] = a * jax.nn.sigmoid(g)
        _dwconv_block(ext_ref, zs_ref, dw_ref, dwb_ref, c_ref, width, hb, tt, c0, cb)
    half = max(rch, tt // 2)
    for r0 in range(0, tt, half):
        for r in range(r0, r0 + half, rch):
            c = c_ref[r:r + rch, :]
            cc = c - jnp.mean(c, axis=-1, keepdims=True)
            y = cc * lax.rsqrt(jnp.mean(cc * cc, axis=-1, keepdims=True) + EPS) * lng_ref[...] + lnb_ref[...]
            y_ref[r:r + rch, :] = _silu(y).astype(BF16)
        out = jnp.dot(y_ref[r0:r0 + half, :], w2_ref[...], preferred_element_type=F32)
        for r in range(0, half, rch):
            o = x_ref[0, r0 + r:r0 + r + rch, :] + _rms(out[r:r + rch] + b2_ref[...], gpost_ref[...])
            o_ref[0, r0 + r:r0 + r + rch, :] = o
            hn_ref[0, r0 + r:r0 + r + rch, :] = _rms(o, gn_ref[...]).astype(BF16)
    tail_ref[0] = ext_ref[tt:tt + hb, :]


def _const_spec(shape):
    nd = len(shape)
    return pl.BlockSpec(shape, lambda b, i: (0,) * nd)


def _layer_spec(shape, layer):
    return pl.BlockSpec((None,) + shape, lambda b, i: (layer,) + (0,) * len(shape), pipeline_mode=pl.Buffered(1))


def conformer_mixer(x, gpre, w1, b1, cache, dw, dwb, lng, lnb, w2, layer, b2, gpost, g_next, tt=256, cb=256):
    bsz, t, d = x.shape
    width = dw.shape[0]
    hb = 32
    tt = _pick_tile(t, tt)
    cb = _pick_tile(d, cb)
    row = lambda a: a.reshape(1, -1)
    xspec = pl.BlockSpec((1, tt, d), lambda b, i: (b, i, 0))
    out, tail, hn = pl.pallas_call(
        functools.partial(_conformer_kernel, width=width, hb=hb, tt=tt, d=d, cb=cb),
        grid=(bsz, t // tt),
        in_specs=[xspec, _const_spec((1, d)), _layer_spec((d, 2 * d), layer), _const_spec((1, 2 * d)),
                  pl.BlockSpec((1, hb, d), lambda b, i: (b, 0, 0)), _const_spec((hb, d)), _const_spec((1, d)),
                  _const_spec((1, d)), _const_spec((1, d)), _layer_spec((d, d), layer),
                  _const_spec((1, d)), _const_spec((1, d)), _const_spec((1, d))],
        out_specs=[xspec, pl.BlockSpec((1, hb, d), lambda b, i: (b, 0, 0)), xspec],
        out_shape=[jax.ShapeDtypeStruct((bsz, t, d), F32), jax.ShapeDtypeStruct((bsz, hb, d), F32),
                   jax.ShapeDtypeStruct((bsz, t, d), BF16)],
        scratch_shapes=[pltpu.VMEM((hb + tt + SUBLANES, d), F32), pltpu.VMEM((SUBLANES, tt + SUBLANES, cb), F32),
                        pltpu.VMEM((tt, d), BF16), pltpu.VMEM((tt, d), F32), pltpu.VMEM((tt, d), BF16)],
        compiler_params=_cparams(("parallel", "arbitrary")),
        name="conformer_mixer",
    )(x, row(gpre), w1, row(b1), _pad_rows(cache, hb), jnp.pad(dw, ((0, hb - width), (0, 0))), row(dwb),
      row(lng), row(lnb), w2, row(b2), row(gpost), row(g_next))
    return out, tail[:, hb - (width - 1):], hn


def _sconv_tail_kernel(gx_ref, cache_ref, bg_ref, cw_ref, w_ref, gpost_ref, gn_ref, x_ref, o_ref, hn_ref, ext_ref,
                       zs_ref, c_ref, y_ref, *, width, hb, tt, d, cb):
    _carry_ext(ext_ref, cache_ref, hb, tt)
    ext_ref[hb:hb + tt, :] = gx_ref[0]
    _dwconv(ext_ref, zs_ref, cw_ref, c_ref, width, hb, tt, d, cb)
    rch = min(ROW_CHUNK, tt)
    for r in range(0, tt, rch):
        y_ref[r:r + rch, :] = (bg_ref[0, r:r + rch, :] * c_ref[r:r + rch, :]).astype(BF16)
    out = jnp.dot(y_ref[...], w_ref[...], preferred_element_type=F32)
    for r in range(0, tt, rch):
        o = x_ref[0, r:r + rch, :] + _rms(out[r:r + rch], gpost_ref[...])
        o_ref[0, r:r + rch, :] = o
        hn_ref[0, r:r + rch, :] = _rms(o, gn_ref[...]).astype(BF16)


def sconv_tail(gx, cache, bg, cw, w_out, layer, gpost, g_next, x, tt=256, cb=512):
    bsz, t, d = gx.shape
    width = cw.shape[0]
    hb = SUBLANES
    tt = _pick_tile(t, tt)
    cb = _pick_tile(d, cb)
    cur = pl.BlockSpec((1, tt, d), lambda b, i: (b, i, 0))
    return pl.pallas_call(
        functools.partial(_sconv_tail_kernel, width=width, hb=hb, tt=tt, d=d, cb=cb),
        grid=(bsz, t // tt),
        in_specs=[cur, pl.BlockSpec((1, hb, d), lambda b, i: (b, 0, 0)), cur, _const_spec((hb, d)),
                  _layer_spec((d, d), layer), _const_spec((1, d)), _const_spec((1, d)), cur],
        out_specs=[cur, cur],
        out_shape=[jax.ShapeDtypeStruct((bsz, t, d), F32), jax.ShapeDtypeStruct((bsz, t, d), BF16)],
        scratch_shapes=[pltpu.VMEM((hb + tt + SUBLANES, d), F32), pltpu.VMEM((SUBLANES, tt + SUBLANES, cb), F32),
                        pltpu.VMEM((tt, d), F32), pltpu.VMEM((tt, d), BF16)],
        compiler_params=_cparams(("parallel", "arbitrary")),
        name="sconv_tail",
    )(gx, _pad_rows(cache, hb), bg, jnp.pad(cw, ((0, hb - width), (0, 0))), w_out, gpost.reshape(1, d),
      g_next.reshape(1, d), x)


def _delta_kernel(*refs, hp, L, has_state):
    hd = DN_HEAD_DIM
    if has_state:
        (q_ref, k_ref, v_ref, z_ref, ba_ref, cq_ref, ck_ref, cv_ref, wq_ref, wk_ref, wv_ref, par_ref, ng_ref,
         s0_ref, o_ref, s_ref, eq_ref, ek_ref, ev_ref) = refs
    else:
        (q_ref, k_ref, v_ref, z_ref, ba_ref, cq_ref, ck_ref, cv_ref, wq_ref, wk_ref, wv_ref, par_ref, ng_ref,
         o_ref, s_ref, eq_ref, ek_ref, ev_ref) = refs
        s0_ref = None
    hb = SUBLANES
    width = 4
    L2 = 2 * L

    @pl.when(pl.program_id(2) == 0)
    def _():
        eq_ref[0:hb, :] = cq_ref[0]
        ek_ref[0:hb, :] = ck_ref[0]
        ev_ref[0:hb, :] = cv_ref[0]
        if has_state:
            s_ref[...] = s0_ref[...]
        else:
            s_ref[...] = jnp.zeros(s_ref.shape, F32)

    eq_ref[hb:hb + L, :] = q_ref[0]
    ek_ref[hb:hb + L, :] = k_ref[0]
    ev_ref[hb:hb + L, :] = v_ref[0]

    def conv_silu(e_ref, w_ref, lo):
        acc = None
        for kk in range(width):
            r0 = hb - (width - 1) + kk
            term = e_ref[r0:r0 + L, lo:lo + hd] * w_ref[kk:kk + 1, lo:lo + hd]
            acc = term if acc is None else acc + term
        return _silu(acc)

    ba = ba_ref[0]
    beta_all = jax.nn.sigmoid(ba)
    g_all = -jnp.exp(par_ref[0, 0:1, :]) * jax.nn.softplus(ba + par_ref[0, 1:2, :])
    ri = lax.broadcasted_iota(jnp.int32, (L, L), 0)
    ci = lax.broadcasted_iota(jnp.int32, (L, L), 1)
    gc_all = _mm_f32(jnp.where(ri >= ci, 1.0, 0.0).astype(F32), g_all)
    gc_t = gc_all.T
    grow_all = jnp.concatenate([gc_t[64:64 + hp], gc_t[96:96 + hp]], axis=1)
    lane_h = lax.broadcasted_iota(jnp.int32, (hp, L2), 1)
    gl_e = jnp.broadcast_to(grow_all[:, L - 1:L], (hp, L2))
    gl_o = jnp.broadcast_to(grow_all[:, L2 - 1:L2], (hp, L2))
    kdec_all = jnp.exp(jnp.where(lane_h < L, gl_e, gl_o) - grow_all)
    egl_e = jnp.exp(gl_e)
    egl_o = jnp.exp(gl_o)

    r2 = lax.broadcasted_iota(jnp.int32, (L, L2), 0)
    l2 = lax.broadcasted_iota(jnp.int32, (L, L2), 1)
    c2 = jnp.bitwise_and(l2, L - 1)
    incl2 = r2 >= c2
    strict2 = r2 > c2
    first2 = l2 < L
    eye2 = jnp.where(r2 == c2, 1.0, 0.0).astype(F32)
    rb = lax.broadcasted_iota(jnp.int32, (L2, L2), 0)
    lb = lax.broadcasted_iota(jnp.int32, (L2, L2), 1)
    bmask = (rb < L) == (lb < L)
    zero_sq = jnp.zeros((L, hd), F32)

    def bdiag(x):
        xb = x.astype(BF16)
        return jnp.where(bmask, jnp.concatenate([xb, xb], axis=0), jnp.zeros((), BF16))

    def bcast(col):
        return jnp.broadcast_to(col, (L, hd))

    pairs = range(hp)
    qn, kn, knt2 = [], [], []
    for j in pairs:
        qj = conv_silu(eq_ref, wq_ref, j * hd)
        kj = conv_silu(ek_ref, wk_ref, j * hd)
        qn.append(qj * lax.rsqrt(jnp.sum(qj * qj, axis=-1, keepdims=True) + EPS) * (hd ** -0.5))
        kn.append(kj * lax.rsqrt(jnp.sum(kj * kj, axis=-1, keepdims=True) + EPS))
        knt2.append(jnp.concatenate([kn[j], kn[j]], axis=0).T)

    a2 = [_mm(jnp.concatenate([qn[j], kn[j]], axis=0), knt2[j]) for j in pairs]

    bcs, egcs, tmat, pmat, qkm2 = [], [], [], [], []
    for j in pairs:
        gce, gco = bcast(gc_all[:, 64 + j:65 + j]), bcast(gc_all[:, 96 + j:97 + j])
        bce, bco = bcast(beta_all[:, j:j + 1]), bcast(beta_all[:, 32 + j:33 + j])
        bcs.append((bce, bco))
        egcs.append((jnp.exp(gce), jnp.exp(gco)))
        gcol2 = jnp.where(first2, gce, gco)
        bcol2 = jnp.where(first2, bce, bco)
        decay2 = jnp.exp(jnp.where(incl2, gcol2 - grow_all[j:j + 1, :], NEG_INF))
        m2 = jnp.where(strict2, bcol2 * a2[j][L:] * decay2, 0.0)
        qkm2.append(jnp.where(incl2, a2[j][:L] * decay2, 0.0))
        tmat.append(eye2 - m2)
        pmat.append(m2)

    nst = int(math.log2(L)) - 1
    pmat = [_mm(pmat[j], bdiag(pmat[j])) for j in pairs]
    for st in range(nst):
        if st < nst - 1:
            outs = [_mm(jnp.concatenate([tmat[j], pmat[j]], axis=0), bdiag(pmat[j])) for j in pairs]
            tmat = [tmat[j] + outs[j][:L] for j in pairs]
            pmat = [outs[j][L:] for j in pairs]
        else:
            tmat = [tmat[j] + _mm(tmat[j], bdiag(pmat[j])) for j in pairs]

    heads = [(j, r) for j in pairs for r in range(2)]
    sols = []
    for j, r in heads:
        h = 2 * j + r
        vh = conv_silu(ev_ref, wv_ref, h * hd)
        bc_, egc_ = bcs[j][r], egcs[j][r]
        rhs = jnp.concatenate([bc_ * vh, (bc_ * egc_) * kn[j]], axis=1).astype(BF16)
        zr = jnp.zeros_like(rhs)
        rhs_pad = jnp.concatenate([rhs, zr] if r == 0 else [zr, rhs], axis=0)
        sols.append(_mm(tmat[j], rhs_pad))

    xs = []
    for idx, (j, r) in enumerate(heads):
        h = 2 * j + r
        w_ = sols[idx][:, hd:]
        xs.append(_mm(jnp.concatenate([w_, qn[j] * egcs[j][r]], axis=0), s_ref[0, h]))

    kgt2 = [knt2[j] * kdec_all[j:j + 1, :] for j in pairs]
    for idx, (j, r) in enumerate(heads):
        h = 2 * j + r
        v_new = sols[idx][:, :hd] - xs[idx][:L]
        vpad = jnp.concatenate([v_new, zero_sq] if r == 0 else [zero_sq, v_new], axis=0)
        y = _mm(jnp.concatenate([qkm2[j], kgt2[j]], axis=0), vpad)
        egl = (egl_e if r == 0 else egl_o)[j:j + 1, :]
        s_ref[0, h] = s_ref[0, h] * egl + y[L:]
        o = xs[idx][L:] + y[:L]
        zh = z_ref[0, :, h * hd:(h + 1) * hd]
        o = o * lax.rsqrt(jnp.mean(o * o, axis=-1, keepdims=True) + EPS) * ng_ref[...] * _silu(zh)
        o_ref[0, :, h * hd:(h + 1) * hd] = o.astype(o_ref.dtype)

    eq_ref[0:hb, :] = eq_ref[L:L + hb, :]
    ek_ref[0:hb, :] = ek_ref[L:L + hb, :]
    ev_ref[0:hb, :] = ev_ref[L:L + hb, :]


def delta_rule(qkv, z, ba, conv_cache, conv_w, par, norm_g, s0, hp):
    bsz, t, _ = qkv.shape
    vdim = z.shape[-1]
    hd = DN_HEAD_DIM
    assert 2 * CHUNK == hd, "head pairs are packed into one lane tile"
    vh = vdim // hd
    qh = vh // 2
    hg = qh // hp
    nh = 2 * hp
    L = CHUNK
    nc = t // L
    hb = SUBLANES
    cache = _pad_rows(conv_cache, hb)
    cw = jnp.pad(conv_w, ((0, hb - conv_w.shape[0]), (0, 0)))
    qw, vw = hp * hd, nh * hd
    koff, voff = qh // hp, 2 * qh // nh
    in_specs = [
        pl.BlockSpec((1, L, qw), lambda b, g, c: (b, c, g)),
        pl.BlockSpec((1, L, qw), lambda b, g, c: (b, c, koff + g)),
        pl.BlockSpec((1, L, vw), lambda b, g, c: (b, c, voff + g)),
        pl.BlockSpec((1, L, vw), lambda b, g, c: (b, c, g)),
        pl.BlockSpec((1, L, LANES), lambda b, g, c: (b, c, g)),
        pl.BlockSpec((1, hb, qw), lambda b, g, c: (b, 0, g)),
        pl.BlockSpec((1, hb, qw), lambda b, g, c: (b, 0, koff + g)),
        pl.BlockSpec((1, hb, vw), lambda b, g, c: (b, 0, voff + g)),
        pl.BlockSpec((hb, qw), lambda b, g, c: (0, g)),
        pl.BlockSpec((hb, qw), lambda b, g, c: (0, koff + g)),
        pl.BlockSpec((hb, vw), lambda b, g, c: (0, voff + g)),
        pl.BlockSpec((1, hb, LANES), lambda b, g, c: (g, 0, 0)),
        pl.BlockSpec((1, hd), lambda b, g, c: (0, 0)),
    ]
    args = [qkv, qkv, qkv, z, ba, cache, cache, cache, cw, cw, cw, par, norm_g.reshape(1, hd)]
    if s0 is not None:
        in_specs.append(pl.BlockSpec((1, nh, hd, hd), lambda b, g, c: (b, g, 0, 0)))
        args.append(s0)
    o, s = pl.pallas_call(
        functools.partial(_delta_kernel, hp=hp, L=L, has_state=s0 is not None),
        grid=(bsz, hg, nc),
        in_specs=in_specs,
        out_specs=[pl.BlockSpec((1, L, vw), lambda b, g, c: (b, c, g)),
                   pl.BlockSpec((1, nh, hd, hd), lambda b, g, c: (b, g, 0, 0))],
        out_shape=[jax.ShapeDtypeStruct((bsz, t, vdim), BF16),
                   jax.ShapeDtypeStruct((bsz, vh, hd, hd), F32)],
        scratch_shapes=[pltpu.VMEM((hb + L, qw), F32), pltpu.VMEM((hb + L, qw), F32),
                        pltpu.VMEM((hb + L, vw), F32)],
        compiler_params=_cparams(("parallel", "parallel", "arbitrary")),
        name="delta_rule",
    )(*args)
    return o, s


def _attn_kernel(q_ref, k0_ref, k1_ref, k2_ref, v0_ref, v1_ref, v2_ref, bias_ref, o_ref, *, nkv, grp, masked):
    hd = SWA_HEAD_DIM
    pw = 2 * hd
    npair = grp // 2
    nq = q_ref.shape[1]
    nk = k0_ref.shape[1] * 3
    nkp = bias_ref.shape[-1]
    c = pl.program_id(1)
    q = q_ref[0]
    zrows = jnp.zeros((nkp - nk, k0_ref.shape[2]), F32)
    k = jnp.concatenate([k0_ref[0], k1_ref[0], k2_ref[0], zrows], axis=0)
    v = jnp.concatenate([v0_ref[0], v1_ref[0], v2_ref[0], zrows], axis=0)
    low = lax.broadcasted_iota(jnp.int32, (nkp, pw), 1) < hd
    if masked:
        col = lax.broadcasted_iota(jnp.int32, (npair * nq, nkp), 1)
        valid = jnp.logical_or(col >= nk, c * CHUNK - WINDOW + col >= 0)
    ones = jnp.ones((nkp, pw), BF16)

    def halves(x, n):
        blk = x[:, (n // 2) * pw:(n // 2 + 1) * pw]
        swp = pltpu.roll(blk, hd, axis=1)
        lo_src, hi_src = (blk, swp) if n % 2 == 0 else (swp, blk)
        return (jnp.where(low, lo_src, 0.0).astype(BF16), jnp.where(low, 0.0, hi_src).astype(BF16))

    heads = range(nkv)
    kh = [halves(k, n) for n in heads]
    vh = [halves(v, n) for n in heads]
    q2 = [jnp.concatenate([q[:, (n * grp + 2 * a) * hd:(n * grp + 2 * a + 2) * hd] for a in range(npair)],
                          axis=0).astype(BF16) for n in heads]
    units = [(n, par) for n in heads for par in range(2)]
    ss = []
    for n, par in units:
        s = _mm_nt(q2[n], kh[n][par]) * (hd ** -0.5) + bias_ref[n, par]
        ss.append(jnp.where(valid, s, NEG_INF) if masked else s)
    ps = [jnp.exp(s - jnp.max(s, axis=-1, keepdims=True)).astype(BF16) for s in ss]
    dens = [jnp.dot(p, ones, preferred_element_type=F32) for p in ps]
    avs = [jnp.dot(ps[i], vh[n][par], preferred_element_type=F32) for i, (n, par) in enumerate(units)]
    for n in heads:
        o2 = avs[2 * n] / dens[2 * n] + avs[2 * n + 1] / dens[2 * n + 1]
        for a in range(npair):
            o_ref[0, :, (n * grp + 2 * a) * hd:(n * grp + 2 * a + 2) * hd] = o2[a * nq:(a + 1) * nq].astype(o_ref.dtype)


def _t5_bucket(rel):
    half = REL_BUCKETS // 2
    max_exact = half // 2
    a = jnp.abs(rel)
    af = jnp.maximum(a, 1).astype(F32)
    large = max_exact + (jnp.log(af / max_exact) / math.log(REL_MAX_DIST / max_exact)
                         * (half - max_exact)).astype(jnp.int32)
    large = jnp.minimum(large, half - 1)
    return jnp.where(rel > 0, half, 0) + jnp.where(a < max_exact, a, large)


def _bias_table(rel_bias, sinks, nkv, grp, n_q, n_k, n_kp):
    npair = grp // 2
    rel = jnp.arange(n_k)[None, :] - WINDOW - jnp.arange(n_q)[:, None]
    onehot = (_t5_bucket(rel)[..., None] == jnp.arange(rel_bias.shape[0])).astype(F32)
    bias = jnp.einsum("qkb,bh->qkh", onehot, rel_bias.astype(F32), precision=lax.Precision.HIGHEST)
    bias = jnp.transpose(bias, (2, 0, 1)).reshape(nkv, npair, 2, n_q, n_k)
    bias = jnp.transpose(bias, (0, 2, 1, 3, 4)).reshape(nkv, 2, npair * n_q, n_k)
    sink = jnp.transpose(sinks.astype(F32).reshape(nkv, npair, 2), (0, 2, 1))
    sink = jnp.repeat(sink, n_q, axis=2).reshape(nkv, 2, npair * n_q, 1)
    pad = jnp.full((nkv, 2, npair * n_q, n_kp - n_k - 1), NEG_INF, F32)
    return jnp.concatenate([bias, sink, pad], axis=-1)


def swa_attention(q_src, k_src, v_src, kv_col, rel_bias, sinks, nkv, masked):
    hd = SWA_HEAD_DIM
    nheads = sinks.shape[0]
    grp = nheads // nkv
    assert grp % 2 == 0 and nkv % 2 == 0, "heads are processed in lane-tile pairs"
    qd, kvd = nheads * hd, nkv * hd
    bsz = q_src.shape[0]
    t = k_src.shape[1] if masked else k_src.shape[1] - WINDOW
    nc = t // CHUNK
    nkp = 2 * LANES
    bias = _bias_table(rel_bias, sinks, nkv, grp, CHUNK, WINDOW + CHUNK, nkp)
    if masked:
        rows = [lambda b, c, j=j: jnp.maximum(c + j - 2, 0) for j in range(3)]
    else:
        rows = [lambda b, c, j=j: c + j for j in range(3)]
    kspecs = [pl.BlockSpec((1, CHUNK, kvd), lambda b, c, r=r: (b, r(b, c), kv_col[0])) for r in rows]
    vspecs = [pl.BlockSpec((1, CHUNK, kvd), lambda b, c, r=r: (b, r(b, c), kv_col[1])) for r in rows]
    return pl.pallas_call(
        functools.partial(_attn_kernel, nkv=nkv, grp=grp, masked=masked),
        grid=(bsz, nc),
        in_specs=[pl.BlockSpec((1, CHUNK, qd), lambda b, c: (b, c, 0))] + kspecs + vspecs
        + [pl.BlockSpec((nkv, 2, grp // 2 * CHUNK, nkp), lambda b, c: (0, 0, 0, 0))],
        out_specs=pl.BlockSpec((1, CHUNK, qd), lambda b, c: (b, c, 0)),
        out_shape=jax.ShapeDtypeStruct((bsz, t, qd), BF16),
        compiler_params=_cparams(("parallel", "arbitrary")),
        name="swa_attention",
    )(q_src, k_src, k_src, k_src, v_src, v_src, v_src, bias)


def _delta_heads_per_step(qh):
    return min(16, qh)


def _run_group(x, caches, w, wb, first_chunk):
    conv_a, delta_s, delta_conv, sconv, swa_k, swa_v = caches
    bsz, t, d = x.shape
    depth = w["norm_mix_pre"].shape[0]
    xf = x.reshape(bsz * t, d)
    new = {k: [] for k in ("conv_a", "ds", "dc", "sc", "k", "v")}
    for i in range(depth):
        mix, j = i % 4, i // 4
        g_pre, g_post, g_ffn = w["norm_mix_pre"][i], w["norm_mix_post"][i], w["norm_ffn_pre"][i]
        if mix == 0:
            width = w["conv_a_dw"].shape[1]
            cache = conv_a[j] if conv_a is not None else jnp.zeros((bsz, width - 1, d), F32)
            x3, tail, hf = conformer_mixer(xf.reshape(bsz, t, d), g_pre, wb["conv_a_w1"], w["conv_a_b1"][j],
                                           cache, w["conv_a_dw"][j], w["conv_a_dw_b"][j], w["conv_a_ln_g"][j],
                                           w["conv_a_ln_b"][j], wb["conv_a_w2"], j, w["conv_a_b2"][j], g_post,
                                           g_ffn)
            xf, hf = x3.reshape(bsz * t, d), hf.reshape(bsz * t, d)
            new["conv_a"].append(tail)
        elif mix == 1:
            w_in = w["delta_w_in"][j]
            vh = w["delta_a_log"].shape[1]
            vdim = vh * DN_HEAD_DIM
            qkvd = w["delta_conv_w"].shape[2]
            qh = (qkvd - vdim) // (2 * DN_HEAD_DIM)
            hp = _delta_heads_per_step(qh)
            hg = qh // hp
            (qkv,) = matmul_cols(hm, [(wb["delta_w_in"], j, 0)], qkvd, _comb_id, [F32])
            (z,) = matmul_cols(hm, [(wb["delta_w_in"], j, qkvd)], vdim, _comb_id, [F32])

            def lanes4(be, bo, ae, ao):
                pad = [(0, 0)] * (be.ndim - 1) + [(0, 32 - hp)]
                return jnp.concatenate([jnp.pad(a, pad) for a in (be, bo, ae, ao)], axis=-1)

            w_b = w_in[:, qkvd + vdim:qkvd + vdim + vh].reshape(d, hg, hp, 2)
            w_a = w_in[:, qkvd + vdim + vh:].reshape(d, hg, hp, 2)
            w_ba = lanes4(w_b[..., 0], w_b[..., 1], w_a[..., 0], w_a[..., 1]).reshape(d, hg * LANES)
            (ba,) = matmul_cols(hm, [(w_ba.astype(BF16)[None], 0, 0)], hg * LANES, _comb_id, [F32], tn=LANES)
            zl = jnp.zeros((hg, hp), F32)
            alog = w["delta_a_log"][j].reshape(hg, hp, 2)
            dtb = w["delta_dt_bias"][j].reshape(hg, hp, 2)
            par = jnp.stack([lanes4(zl, zl, alog[..., 0], alog[..., 1]),
                             lanes4(zl, zl, dtb[..., 0], dtb[..., 1])], axis=1)
            par = jnp.pad(par, ((0, 0), (0, SUBLANES - 2), (0, 0)))
            cw = w["delta_conv_w"][j]
            cache = delta_conv[j] if delta_conv is not None else jnp.zeros((bsz, cw.shape[0] - 1, qkvd), F32)
            qkv3 = qkv.reshape(bsz, t, qkvd)
            o, s_new = delta_rule(qkv3, z.reshape(bsz, t, vdim), ba.reshape(bsz, t, hg * LANES), cache, cw, par,
                                  w["delta_norm_g"][j], delta_s[j] if delta_s is not None else None, hp)
            xf, hf = matmul_norm_residual(o.reshape(bsz * t, vdim), wb["delta_w_out"], j, g_post, xf, g_ffn)
            new["ds"].append(s_new)
            new["dc"].append(qkv3[:, -(cw.shape[0] - 1):])
        elif mix == 2:
            w_in = wb["sconv_w_in"]
            bg, gx = matmul_cols(hm, [(w_in, j, 0), (w_in, j, d), (w_in, j, 2 * d)], d, _comb_sconv, [F32, F32],
                                 tm=1024, tn=512)
            gx = gx.reshape(bsz, t, d)
            cw = w["sconv_w"][j]
            cache = sconv[j] if sconv is not None else jnp.zeros((bsz, cw.shape[0] - 1, d), F32)
            x3, hf = sconv_tail(gx, cache, bg.reshape(bsz, t, d), cw, wb["sconv_w_out"], j, g_post, g_ffn,
                                xf.reshape(bsz, t, d))
            xf, hf = x3.reshape(bsz * t, d), hf.reshape(bsz * t, d)
            new["sc"].append(gx[:, -(cw.shape[0] - 1):])
        else:
            nheads = w["swa_sinks"].shape[1]
            qd = nheads * SWA_HEAD_DIM
            kvd = (w["swa_w_qkv"].shape[2] - qd) // 2
            nkv = kvd // SWA_HEAD_DIM
            (qkv,) = matmul_cols(hm, [(wb["swa_w_qkv"], j, 0)], qd + 2 * kvd, _comb_id, [F32])
            qkv = qkv.reshape(bsz, t, qd + 2 * kvd)
            if first_chunk:
                o = swa_attention(qkv, qkv, qkv, (qd // kvd, qd // kvd + 1), w["rel_bias"], w["swa_sinks"][j],
                                  nkv, True)
                k_ext, v_ext = qkv[:, :, qd:qd + kvd], qkv[:, :, qd + kvd:]
            else:
                k_ext = jnp.concatenate([swa_k[j].reshape(bsz, WINDOW, kvd), qkv[:, :, qd:qd + kvd]], axis=1)
                v_ext = jnp.concatenate([swa_v[j].reshape(bsz, WINDOW, kvd), qkv[:, :, qd + kvd:]], axis=1)
                o = swa_attention(qkv, k_ext, v_ext, (0, 0), w["rel_bias"], w["swa_sinks"][j], nkv, False)
            xf, hf = matmul_norm_residual(o.reshape(bsz * t, qd), wb["swa_w_out"], j, g_post, xf, g_ffn)
            new["k"].append(k_ext[:, -WINDOW:].reshape(bsz, WINDOW, nkv, SWA_HEAD_DIM))
            new["v"].append(v_ext[:, -WINDOW:].reshape(bsz, WINDOW, nkv, SWA_HEAD_DIM))
        g_next = w["norm_mix_pre"][i + 1] if i + 1 < depth else None
        xf, hm = ffn(hf, xf, wb["ffn_w_gate"], wb["ffn_w_up"], wb["ffn_w_down"], i, w["norm_ffn_post"][i], g_next)
    return xf.reshape(bsz, t, d), tuple(jnp.stack(new[k]) for k in ("conv_a", "ds", "dc", "sc", "k", "v"))


def kernel(x_prompt, x_sample, cache_conv_a, state_delta_s, state_delta_conv, cache_sconv, cache_swa_k, cache_swa_v, rel_bias, norm_mix_pre, norm_mix_post, norm_ffn_pre, norm_ffn_post, ffn_w_gate, ffn_w_up, ffn_w_down, conv_a_w1, conv_a_b1, conv_a_dw, conv_a_dw_b, conv_a_ln_g, conv_a_ln_b, conv_a_w2, conv_a_b2, delta_w_in, delta_conv_w, delta_a_log, delta_dt_bias, delta_norm_g, delta_w_out, sconv_w_in, sconv_w, sconv_w_out, swa_w_qkv, swa_sinks, swa_w_out):
    w = {
        "rel_bias": rel_bias, "norm_mix_pre": norm_mix_pre, "norm_mix_post": norm_mix_post,
        "norm_ffn_pre": norm_ffn_pre, "norm_ffn_post": norm_ffn_post, "ffn_w_gate": ffn_w_gate,
        "ffn_w_up": ffn_w_up, "ffn_w_down": ffn_w_down, "conv_a_w1": conv_a_w1, "conv_a_b1": conv_a_b1,
        "conv_a_dw": conv_a_dw, "conv_a_dw_b": conv_a_dw_b, "conv_a_ln_g": conv_a_ln_g,
        "conv_a_ln_b": conv_a_ln_b, "conv_a_w2": conv_a_w2, "conv_a_b2": conv_a_b2,
        "delta_w_in": delta_w_in, "delta_conv_w": delta_conv_w, "delta_a_log": delta_a_log,
        "delta_dt_bias": delta_dt_bias, "delta_norm_g": delta_norm_g, "delta_w_out": delta_w_out,
        "sconv_w_in": sconv_w_in, "sconv_w": sconv_w, "sconv_w_out": sconv_w_out,
        "swa_w_qkv": swa_w_qkv, "swa_sinks": swa_sinks, "swa_w_out": swa_w_out,
    }
    wb = {name: w[name].astype(BF16) for name in (
        "ffn_w_gate", "ffn_w_up", "ffn_w_down", "conv_a_w1", "conv_a_w2", "delta_w_in", "delta_w_out",
        "sconv_w_in", "sconv_w_out", "swa_w_qkv", "swa_w_out")}
    y_p, (ca_p, ds_p, dc_p, sc_p, k_p, v_p) = _run_group(x_prompt, (None,) * 6, w, wb, True)
    y_s, (ca_s, ds_s, dc_s, sc_s, k_s, v_s) = _run_group(
        x_sample, (cache_conv_a, state_delta_s, state_delta_conv, cache_sconv, cache_swa_k, cache_swa_v), w, wb,
        False)
    return (y_p, y_s, ca_p, ca_s, ds_p, ds_s, dc_p, dc_s, sc_p, sc_s, k_p, k_s, v_p, v_s)
```

```python
import functools
import math

import jax
import jax.numpy as jnp
from jax import lax
from jax.experimental import pallas as pl
from jax.experimental.pallas import tpu as pltpu

F32 = jnp.float32
BF16 = jnp.bfloat16
EPS = 1e-6
CHUNK = 64
WINDOW = 128
SWA_HEAD_DIM = 64
DN_HEAD_DIM = 128
REL_BUCKETS = 32
REL_MAX_DIST = 128
V7X_VMEM_BUDGET = 56 * 1024 * 1024
SUBLANES = 8
LANES = 128
NEG_INF = float("-inf")


def _cparams(sem, vmem=V7X_VMEM_BUDGET):
    return pltpu.CompilerParams(dimension_semantics=sem, vmem_limit_bytes=vmem)


def _rms(x, g):
    return x * lax.rsqrt(jnp.mean(x * x, axis=-1, keepdims=True) + EPS) * g


def _silu(x):
    return x * jax.nn.sigmoid(x)


def _mm(a, b):
    return jnp.dot(a.astype(BF16), b.astype(BF16), preferred_element_type=F32)


def _mm_nt(a, b):
    return lax.dot_general(a.astype(BF16), b.astype(BF16), (((1,), (1,)), ((), ())),
                           preferred_element_type=F32)


def _mm_f32(a, b):
    return jnp.dot(a, b, preferred_element_type=F32, precision=lax.Precision.HIGHEST)


def _pick_tile(n, pref):
    t = min(n, pref)
    while n % t:
        t //= 2
    return t


def _pad_rows(a, rows):
    pad = rows - a.shape[-2]
    cfg = [(0, 0)] * a.ndim
    cfg[-2] = (pad, 0)
    return jnp.pad(a, cfg)


def _mm_cols_kernel(*refs, n_w, combine, n_out):
    h = refs[0][...]
    ds = [jnp.dot(h, refs[1 + i][...], preferred_element_type=F32) for i in range(n_w)]
    for o_ref, o in zip(refs[1 + n_w:1 + n_w + n_out], combine(*ds)):
        o_ref[...] = o.astype(o_ref.dtype)


def matmul_cols(h, ws, n, combine, out_dtypes, tm=2048, tn=1024):
    t, k = h.shape
    tm = _pick_tile(t, tm)
    tn = _pick_tile(n, tn)
    assert all(c0 % tn == 0 for _, _, c0 in ws)
    w_specs = [pl.BlockSpec((None, k, tn), lambda i, j, layer=layer, cb=c0 // tn: (layer, 0, cb + j))
               for _, layer, c0 in ws]
    return pl.pallas_call(
        functools.partial(_mm_cols_kernel, n_w=len(ws), combine=combine, n_out=len(out_dtypes)),
        grid=(t // tm, n // tn),
        in_specs=[pl.BlockSpec((tm, k), lambda i, j: (i, 0))] + w_specs,
        out_specs=[pl.BlockSpec((tm, tn), lambda i, j: (i, j)) for _ in out_dtypes],
        out_shape=[jax.ShapeDtypeStruct((t, n), dt) for dt in out_dtypes],
        compiler_params=_cparams(("parallel", "arbitrary")),
        name="matmul_cols",
    )(h, *[a for a, _, _ in ws])


def _comb_id(d):
    return (d,)


def _comb_sconv(bg, cg, xin):
    return (bg, cg * xin)


def _mm_norm_res_kernel(a_ref, w_ref, g_ref, gn_ref, x_ref, o_ref, hn_ref):
    hr = a_ref.shape[0] // 2
    for r0 in (0, hr):
        d = jnp.dot(a_ref[r0:r0 + hr, :], w_ref[...], preferred_element_type=F32)
        o = x_ref[r0:r0 + hr, :] + _rms(d, g_ref[...])
        o_ref[r0:r0 + hr, :] = o
        hn_ref[r0:r0 + hr, :] = _rms(o, gn_ref[...]).astype(BF16)


def matmul_norm_residual(a, w, layer, g, x, g_next, tm=512):
    t, k = a.shape
    d = w.shape[2]
    tm = _pick_tile(t, tm)
    row = pl.BlockSpec((tm, d), lambda i: (i, 0))
    return pl.pallas_call(
        _mm_norm_res_kernel,
        grid=(t // tm,),
        in_specs=[pl.BlockSpec((tm, k), lambda i: (i, 0)),
                  pl.BlockSpec((None, k, d), lambda i: (layer, 0, 0), pipeline_mode=pl.Buffered(1)),
                  pl.BlockSpec((1, d), lambda i: (0, 0)),
                  pl.BlockSpec((1, d), lambda i: (0, 0)),
                  row],
        out_specs=[row, row],
        out_shape=[jax.ShapeDtypeStruct((t, d), F32), jax.ShapeDtypeStruct((t, d), BF16)],
        compiler_params=_cparams(("parallel",)),
        name="matmul_norm_residual",
    )(a, w, g.reshape(1, d), g_next.reshape(1, d), x)


def _ffn_kernel(*refs, n, nf, nchunk, dchunk, emit_next):
    if emit_next:
        h_ref, x_ref, wg_ref, wu_ref, wd_ref, g2_ref, gn_ref, o_ref, hn_ref, acc_ref = refs
    else:
        h_ref, x_ref, wg_ref, wu_ref, wd_ref, g2_ref, o_ref, acc_ref = refs
    i = pl.program_id(0)
    f = pl.program_id(1)
    slot = lax.rem(i, 2)
    tm, d = acc_ref.shape[1], acc_ref.shape[2]
    rc = tm // nchunk

    @pl.when(jnp.logical_and(i == 0, f == 0))
    def _():
        acc_ref[...] = jnp.zeros(acc_ref.shape, F32)

    def epilogue():
        r0 = pl.multiple_of(jnp.minimum(f, nchunk - 1) * rc, rc)
        o = x_ref[...] + _rms(acc_ref[1 - slot, pl.ds(r0, rc), :], g2_ref[...])
        o_ref[...] = o
        if emit_next:
            hn_ref[...] = _rms(o, gn_ref[...]).astype(BF16)

    @pl.when(i < n)
    def _():
        epilogue()
        hr = tm // 2
        for r0 in range(0, tm, hr):
            h = h_ref[r0:r0 + hr, :]
            gate = jnp.dot(h, wg_ref[...], preferred_element_type=F32)
            up = jnp.dot(h, wu_ref[...], preferred_element_type=F32)
            a = (_silu(gate) * up).astype(BF16)
            for c0 in range(0, d, dchunk):
                prev = jnp.where(f == 0, 0.0, acc_ref[slot, r0:r0 + hr, c0:c0 + dchunk])
                acc_ref[slot, r0:r0 + hr, c0:c0 + dchunk] = prev + jnp.dot(
                    a, wd_ref[:, c0:c0 + dchunk], preferred_element_type=F32)

    @pl.when(i == n)
    def _():
        epilogue()


def ffn(h, x, wg, wu, wd, layer, g2, g_next, tm=1024, tf=512, dchunk=512, nchunk=8):
    t, d = x.shape
    fh = wg.shape[2]
    tm = _pick_tile(t, tm)
    tf = _pick_tile(fh, tf)
    nf = fh // tf
    n = t // tm
    dchunk = _pick_tile(d, dchunk)
    while nchunk > nf:
        nchunk //= 2
    rc = tm // nchunk
    emit_next = g_next is not None
    chunk = pl.BlockSpec(
        (rc, d), lambda i, f: (jnp.where(i == 0, 0, (i - 1) * nchunk + jnp.minimum(f, nchunk - 1)), 0))
    vec = pl.BlockSpec((1, d), lambda i, f: (0, 0))
    wcol = lambda i, f: jnp.where(i == n, nf - 1, f)
    in_specs = [pl.BlockSpec((tm, d), lambda i, f: (jnp.minimum(i, n - 1), 0)), chunk,
                pl.BlockSpec((None, d, tf), lambda i, f: (layer, 0, wcol(i, f))),
                pl.BlockSpec((None, d, tf), lambda i, f: (layer, 0, wcol(i, f))),
                pl.BlockSpec((None, tf, d), lambda i, f: (layer, wcol(i, f), 0)), vec]
    args = [h, x, wg, wu, wd, g2.reshape(1, d)]
    out_specs, out_shape = [chunk], [jax.ShapeDtypeStruct((t, d), F32)]
    if emit_next:
        in_specs.append(vec)
        args.append(g_next.reshape(1, d))
        out_specs.append(chunk)
        out_shape.append(jax.ShapeDtypeStruct((t, d), BF16))
    outs = pl.pallas_call(
        functools.partial(_ffn_kernel, n=n, nf=nf, nchunk=nchunk, dchunk=dchunk, emit_next=emit_next),
        grid=(n + 1, nf),
        in_specs=in_specs,
        out_specs=out_specs,
        out_shape=out_shape,
        scratch_shapes=[pltpu.VMEM((2, tm, d), F32)],
        compiler_params=_cparams(("arbitrary", "arbitrary")),
        name="ffn",
    )(*args)
    return (outs[0], outs[1]) if emit_next else (outs[0], None)


def _dwconv(ext_ref, zs_ref, w_ref, out_ref, width, hb, tt, d, cb):
    for c0 in range(0, d, cb):
        _dwconv_block(ext_ref, zs_ref, w_ref, None, out_ref, width, hb, tt, c0, cb)


ROW_CHUNK = 64
COL_CHUNK = 256


def _dwconv_block(ext_ref, zs_ref, w_ref, bias_ref, out_ref, width, hb, tt, c0, cb):
    base = hb - (width - 1)
    taps = {}
    for k in range(width):
        taps.setdefault((base + k) % SUBLANES, []).append(k)
    rows = tt + SUBLANES
    cch = min(COL_CHUNK, cb)
    for cc in range(0, cb, cch):
        for r in range(0, rows, ROW_CHUNK):
            nr = min(ROW_CHUNK, rows - r)
            for s, ks in taps.items():
                acc = None
                for k in ks:
                    r0 = base + k - s + r
                    term = ext_ref[r0:r0 + nr, c0 + cc:c0 + cc + cch] * w_ref[k:k + 1, c0 + cc:c0 + cc + cch]
                    acc = term if acc is None else acc + term
                zs_ref[s, r:r + nr, cc:cc + cch] = acc
    for cc in range(0, cb, cch):
        for r in range(0, tt, ROW_CHUNK):
            nr = min(ROW_CHUNK, tt - r)
            out = None
            for s in taps:
                part = zs_ref[s, s + r:s + r + nr, cc:cc + cch]
                out = part if out is None else out + part
            if bias_ref is not None:
                out = out + bias_ref[:, c0 + cc:c0 + cc + cch]
            out_ref[r:r + nr, c0 + cc:c0 + cc + cch] = out


def _carry_ext(ext_ref, cache_ref, hb, tt):
    @pl.when(pl.program_id(1) == 0)
    def _():
        ext_ref[0:hb, :] = cache_ref[0]
        ext_ref[hb + tt:hb + tt + SUBLANES, :] = jnp.zeros((SUBLANES, ext_ref.shape[1]), F32)

    @pl.when(pl.program_id(1) > 0)
    def _():
        ext_ref[0:hb, :] = ext_ref[tt:tt + hb, :]


def _conformer_kernel(x_ref, gpre_ref, w1_ref, b1_ref, cache_ref, dw_ref, dwb_ref, lng_ref, lnb_ref, w2_ref,
                      b2_ref, gpost_ref, gn_ref, o_ref, tail_ref, hn_ref, ext_ref, zs_ref, h_ref, c_ref, y_ref,
                      *, width, hb, tt, d, cb):
    _carry_ext(ext_ref, cache_ref, hb, tt)
    rch = min(ROW_CHUNK, tt)
    for r in range(0, tt, rch):
        h_ref[r:r + rch, :] = _rms(x_ref[0, r:r + rch, :], gpre_ref[...]).astype(BF16)
    h = h_ref[...]
    for c0 in range(0, d, cb):
        a = jnp.dot(h, w1_ref[:, c0:c0 + cb], preferred_element_type=F32) + b1_ref[:, c0:c0 + cb]
        g = jnp.dot(h, w1_ref[:, d + c0:d + c0 + cb], preferred_element_type=F32) + b1_ref[:, d + c0:d + c0 + cb]
        ext_ref[hb:hb + tt, c0:c0 + cb] = a * jax.nn.sigmoid(g)
        _dwconv_block(ext_ref, zs_ref, dw_ref, dwb_ref, c_ref, width, hb, tt, c0, cb)
    half = max(rch, tt // 2)
    for r0 in range(0, tt, half):
        for r in range(r0, r0 + half, rch):
            c = c_ref[r:r + rch, :]
            cc = c - jnp.mean(c, axis=-1, keepdims=True)
            y = cc * lax.rsqrt(jnp.mean(cc * cc, axis=-1, keepdims=True) + EPS) * lng_ref[...] + lnb_ref[...]
            y_ref[r:r + rch, :] = _silu(y).astype(BF16)
        out = jnp.dot(y_ref[r0:r0 + half, :], w2_ref[...], preferred_element_type=F32)
        for r in range(0, half, rch):
            o = x_ref[0, r0 + r:r0 + r + rch, :] + _rms(out[r:r + rch] + b2_ref[...], gpost_ref[...])
            o_ref[0, r0 + r:r0 + r + rch, :] = o
            hn_ref[0, r0 + r:r0 + r + rch, :] = _rms(o, gn_ref[...]).astype(BF16)
    tail_ref[0] = ext_ref[tt:tt + hb, :]


def _const_spec(shape):
    nd = len(shape)
    return pl.BlockSpec(shape, lambda b, i: (0,) * nd)


def _layer_spec(shape, layer):
    return pl.BlockSpec((None,) + shape, lambda b, i: (layer,) + (0,) * len(shape), pipeline_mode=pl.Buffered(1))


def conformer_mixer(x, gpre, w1, b1, cache, dw, dwb, lng, lnb, w2, layer, b2, gpost, g_next, tt=256, cb=256):
    bsz, t, d = x.shape
    width = dw.shape[0]
    hb = 32
    tt = _pick_tile(t, tt)
    cb = _pick_tile(d, cb)
    row = lambda a: a.reshape(1, -1)
    xspec = pl.BlockSpec((1, tt, d), lambda b, i: (b, i, 0))
    out, tail, hn = pl.pallas_call(
        functools.partial(_conformer_kernel, width=width, hb=hb, tt=tt, d=d, cb=cb),
        grid=(bsz, t // tt),
        in_specs=[xspec, _const_spec((1, d)), _layer_spec((d, 2 * d), layer), _const_spec((1, 2 * d)),
                  pl.BlockSpec((1, hb, d), lambda b, i: (b, 0, 0)), _const_spec((hb, d)), _const_spec((1, d)),
                  _const_spec((1, d)), _const_spec((1, d)), _layer_spec((d, d), layer),
                  _const_spec((1, d)), _const_spec((1, d)), _const_spec((1, d))],
        out_specs=[xspec, pl.BlockSpec((1, hb, d), lambda b, i: (b, 0, 0)), xspec],
        out_shape=[jax.ShapeDtypeStruct((bsz, t, d), F32), jax.ShapeDtypeStruct((bsz, hb, d), F32),
                   jax.ShapeDtypeStruct((bsz, t, d), BF16)],
        scratch_shapes=[pltpu.VMEM((hb + tt + SUBLANES, d), F32), pltpu.VMEM((SUBLANES, tt + SUBLANES, cb), F32),
                        pltpu.VMEM((tt, d), BF16), pltpu.VMEM((tt, d), F32), pltpu.VMEM((tt, d), BF16)],
        compiler_params=_cparams(("parallel", "arbitrary")),
        name="conformer_mixer",
    )(x, row(gpre), w1, row(b1), _pad_rows(cache, hb), jnp.pad(dw, ((0, hb - width), (0, 0))), row(dwb),
      row(lng), row(lnb), w2, row(b2), row(gpost), row(g_next))
    return out, tail[:, hb - (width - 1):], hn


def _sconv_tail_kernel(gx_ref, cache_ref, bg_ref, cw_ref, w_ref, gpost_ref, gn_ref, x_ref, o_ref, hn_ref, ext_ref,
                       zs_ref, c_ref, y_ref, *, width, hb, tt, d, cb):
    _carry_ext(ext_ref, cache_ref, hb, tt)
    ext_ref[hb:hb + tt, :] = gx_ref[0]
    _dwconv(ext_ref, zs_ref, cw_ref, c_ref, width, hb, tt, d, cb)
    rch = min(ROW_CHUNK, tt)
    for r in range(0, tt, rch):
        y_ref[r:r + rch, :] = (bg_ref[0, r:r + rch, :] * c_ref[r:r + rch, :]).astype(BF16)
    out = jnp.dot(y_ref[...], w_ref[...], preferred_element_type=F32)
    for r in range(0, tt, rch):
        o = x_ref[0, r:r + rch, :] + _rms(out[r:r + rch], gpost_ref[...])
        o_ref[0, r:r + rch, :] = o
        hn_ref[0, r:r + rch, :] = _rms(o, gn_ref[...]).astype(BF16)


def sconv_tail(gx, cache, bg, cw, w_out, layer, gpost, g_next, x, tt=256, cb=512):
    bsz, t, d = gx.shape
    width = cw.shape[0]
    hb = SUBLANES
    tt = _pick_tile(t, tt)
    cb = _pick_tile(d, cb)
    cur = pl.BlockSpec((1, tt, d), lambda b, i: (b, i, 0))
    return pl.pallas_call(
        functools.partial(_sconv_tail_kernel, width=width, hb=hb, tt=tt, d=d, cb=cb),
        grid=(bsz, t // tt),
        in_specs=[cur, pl.BlockSpec((1, hb, d), lambda b, i: (b, 0, 0)), cur, _const_spec((hb, d)),
                  _layer_spec((d, d), layer), _const_spec((1, d)), _const_spec((1, d)), cur],
        out_specs=[cur, cur],
        out_shape=[jax.ShapeDtypeStruct((bsz, t, d), F32), jax.ShapeDtypeStruct((bsz, t, d), BF16)],
        scratch_shapes=[pltpu.VMEM((hb + tt + SUBLANES, d), F32), pltpu.VMEM((SUBLANES, tt + SUBLANES, cb), F32),
                        pltpu.VMEM((tt, d), F32), pltpu.VMEM((tt, d), BF16)],
        compiler_params=_cparams(("parallel", "arbitrary")),
        name="sconv_tail",
    )(gx, _pad_rows(cache, hb), bg, jnp.pad(cw, ((0, hb - width), (0, 0))), w_out, gpost.reshape(1, d),
      g_next.reshape(1, d), x)


def _delta_kernel(*refs, hp, L, has_state):
    hd = DN_HEAD_DIM
    if has_state:
        (q_ref, k_ref, v_ref, z_ref, ba_ref, cq_ref, ck_ref, cv_ref, wq_ref, wk_ref, wv_ref, par_ref, ng_ref,
         s0_ref, o_ref, s_ref, eq_ref, ek_ref, ev_ref) = refs
    else:
        (q_ref, k_ref, v_ref, z_ref, ba_ref, cq_ref, ck_ref, cv_ref, wq_ref, wk_ref, wv_ref, par_ref, ng_ref,
         o_ref, s_ref, eq_ref, ek_ref, ev_ref) = refs
        s0_ref = None
    hb = SUBLANES
    width = 4
    L2 = 2 * L

    @pl.when(pl.program_id(2) == 0)
    def _():
        eq_ref[0:hb, :] = cq_ref[0]
        ek_ref[0:hb, :] = ck_ref[0]
        ev_ref[0:hb, :] = cv_ref[0]
        if has_state:
            s_ref[...] = s0_ref[...]
        else:
            s_ref[...] = jnp.zeros(s_ref.shape, F32)

    eq_ref[hb:hb + L, :] = q_ref[0]
    ek_ref[hb:hb + L, :] = k_ref[0]
    ev_ref[hb:hb + L, :] = v_ref[0]

    def conv_silu(e_ref, w_ref, lo):
        acc = None
        for kk in range(width):
            r0 = hb - (width - 1) + kk
            term = e_ref[r0:r0 + L, lo:lo + hd] * w_ref[kk:kk + 1, lo:lo + hd]
            acc = term if acc is None else acc + term
        return _silu(acc)

    ba = ba_ref[0]
    beta_all = jax.nn.sigmoid(ba)
    g_all = -jnp.exp(par_ref[0, 0:1, :]) * jax.nn.softplus(ba + par_ref[0, 1:2, :])
    ri = lax.broadcasted_iota(jnp.int32, (L, L), 0)
    ci = lax.broadcasted_iota(jnp.int32, (L, L), 1)
    gc_all = _mm_f32(jnp.where(ri >= ci, 1.0, 0.0).astype(F32), g_all)
    gc_t = gc_all.T
    grow_all = jnp.concatenate([gc_t[64:64 + hp], gc_t[96:96 + hp]], axis=1)
    lane_h = lax.broadcasted_iota(jnp.int32, (hp, L2), 1)
    gl_e = jnp.broadcast_to(grow_all[:, L - 1:L], (hp, L2))
    gl_o = jnp.broadcast_to(grow_all[:, L2 - 1:L2], (hp, L2))
    kdec_all = jnp.exp(jnp.where(lane_h < L, gl_e, gl_o) - grow_all)
    egl_e = jnp.exp(gl_e)
    egl_o = jnp.exp(gl_o)

    r2 = lax.broadcasted_iota(jnp.int32, (L, L2), 0)
    l2 = lax.broadcasted_iota(jnp.int32, (L, L2), 1)
    c2 = jnp.bitwise_and(l2, L - 1)
    incl2 = r2 >= c2
    strict2 = r2 > c2
    first2 = l2 < L
    eye2 = jnp.where(r2 == c2, 1.0, 0.0).astype(F32)
    rb = lax.broadcasted_iota(jnp.int32, (L2, L2), 0)
    lb = lax.broadcasted_iota(jnp.int32, (L2, L2), 1)
    bmask = (rb < L) == (lb < L)
    zero_sq = jnp.zeros((L, hd), F32)

    def bdiag(x):
        xb = x.astype(BF16)
        return jnp.where(bmask, jnp.concatenate([xb, xb], axis=0), jnp.zeros((), BF16))

    def bcast(col):
        return jnp.broadcast_to(col, (L, hd))

    pairs = range(hp)
    qn, kn, knt2 = [], [], []
    for j in pairs:
        qj = conv_silu(eq_ref, wq_ref, j * hd)
        kj = conv_silu(ek_ref, wk_ref, j * hd)
        qn.append(qj * lax.rsqrt(jnp.sum(qj * qj, axis=-1, keepdims=True) + EPS) * (hd ** -0.5))
        kn.append(kj * lax.rsqrt(jnp.sum(kj * kj, axis=-1, keepdims=True) + EPS))
        knt2.append(jnp.concatenate([kn[j], kn[j]], axis=0).T)

    a2 = [_mm(jnp.concatenate([qn[j], kn[j]], axis=0), knt2[j]) for j in pairs]

    bcs, egcs, tmat, pmat, qkm2 = [], [], [], [], []
    for j in pairs:
        gce, gco = bcast(gc_all[:, 64 + j:65 + j]), bcast(gc_all[:, 96 + j:97 + j])
        bce, bco = bcast(beta_all[:, j:j + 1]), bcast(beta_all[:, 32 + j:33 + j])
        bcs.append((bce, bco))
        egcs.append((jnp.exp(gce), jnp.exp(gco)))
        gcol2 = jnp.where(first2, gce, gco)
        bcol2 = jnp.where(first2, bce, bco)
        decay2 = jnp.exp(jnp.where(incl2, gcol2 - grow_all[j:j + 1, :], NEG_INF))
        m2 = jnp.where(strict2, bcol2 * a2[j][L:] * decay2, 0.0)
        qkm2.append(jnp.where(incl2, a2[j][:L] * decay2, 0.0))
        tmat.append(eye2 - m2)
        pmat.append(m2)

    nst = int(math.log2(L)) - 1
    pmat = [_mm(pmat[j], bdiag(pmat[j])) for j in pairs]
    for st in range(nst):
        if st < nst - 1:
            outs = [_mm(jnp.concatenate([tmat[j], pmat[j]], axis=0), bdiag(pmat[j])) for j in pairs]
            tmat = [tmat[j] + outs[j][:L] for j in pairs]
            pmat = [outs[j][L:] for j in pairs]
        else:
            tmat = [tmat[j] + _mm(tmat[j], bdiag(pmat[j])) for j in pairs]

    heads = [(j, r) for j in pairs for r in range(2)]
    sols = []
    for j, r in heads:
        h = 2 * j + r
        vh = conv_silu(ev_ref, wv_ref, h * hd)
        bc_, egc_ = bcs[j][r], egcs[j][r]
        rhs = jnp.concatenate([bc_ * vh, (bc_ * egc_) * kn[j]], axis=1).astype(BF16)
        zr = jnp.zeros_like(rhs)
        rhs_pad = jnp.concatenate([rhs, zr] if r == 0 else [zr, rhs], axis=0)
        sols.append(_mm(tmat[j], rhs_pad))

    xs = []
    for idx, (j, r) in enumerate(heads):
        h = 2 * j + r
        w_ = sols[idx][:, hd:]
        xs.append(_mm(jnp.concatenate([w_, qn[j] * egcs[j][r]], axis=0), s_ref[0, h]))

    kgt2 = [knt2[j] * kdec_all[j:j + 1, :] for j in pairs]
    for idx, (j, r) in enumerate(heads):
        h = 2 * j + r
        v_new = sols[idx][:, :hd] - xs[idx][:L]
        vpad = jnp.concatenate([v_new, zero_sq] if r == 0 else [zero_sq, v_new], axis=0)
        y = _mm(jnp.concatenate([qkm2[j], kgt2[j]], axis=0), vpad)
        egl = (egl_e if r == 0 else egl_o)[j:j + 1, :]
        s_ref[0, h] = s_ref[0, h] * egl + y[L:]
        o = xs[idx][L:] + y[:L]
        zh = z_ref[0, :, h * hd:(h + 1) * hd]
        o = o * lax.rsqrt(jnp.mean(o * o, axis=-1, keepdims=True) + EPS) * ng_ref[...] * _silu(zh)
        o_ref[0, :, h * hd:(h + 1) * hd] = o.astype(o_ref.dtype)

    eq_ref[0:hb, :] = eq_ref[L:L + hb, :]
    ek_ref[0:hb, :] = ek_ref[L:L + hb, :]
    ev_ref[0:hb, :] = ev_ref[L:L + hb, :]


def delta_rule(qkv, z, ba, conv_cache, conv_w, par, norm_g, s0, hp):
    bsz, t, _ = qkv.shape
    vdim = z.shape[-1]
    hd = DN_HEAD_DIM
    assert 2 * CHUNK == hd, "head pairs are packed into one lane tile"
    vh = vdim // hd
    qh = vh // 2
    hg = qh // hp
    nh = 2 * hp
    L = CHUNK
    nc = t // L
    hb = SUBLANES
    cache = _pad_rows(conv_cache, hb)
    cw = jnp.pad(conv_w, ((0, hb - conv_w.shape[0]), (0, 0)))
    qw, vw = hp * hd, nh * hd
    koff, voff = qh // hp, 2 * qh // nh
    in_specs = [
        pl.BlockSpec((1, L, qw), lambda b, g, c: (b, c, g)),
        pl.BlockSpec((1, L, qw), lambda b, g, c: (b, c, koff + g)),
        pl.BlockSpec((1, L, vw), lambda b, g, c: (b, c, voff + g)),
        pl.BlockSpec((1, L, vw), lambda b, g, c: (b, c, g)),
        pl.BlockSpec((1, L, LANES), lambda b, g, c: (b, c, g)),
        pl.BlockSpec((1, hb, qw), lambda b, g, c: (b, 0, g)),
        pl.BlockSpec((1, hb, qw), lambda b, g, c: (b, 0, koff + g)),
        pl.BlockSpec((1, hb, vw), lambda b, g, c: (b, 0, voff + g)),
        pl.BlockSpec((hb, qw), lambda b, g, c: (0, g)),
        pl.BlockSpec((hb, qw), lambda b, g, c: (0, koff + g)),
        pl.BlockSpec((hb, vw), lambda b, g, c: (0, voff + g)),
        pl.BlockSpec((1, hb, LANES), lambda b, g, c: (g, 0, 0)),
        pl.BlockSpec((1, hd), lambda b, g, c: (0, 0)),
    ]
    args = [qkv, qkv, qkv, z, ba, cache, cache, cache, cw, cw, cw, par, norm_g.reshape(1, hd)]
    if s0 is not None:
        in_specs.append(pl.BlockSpec((1, nh, hd, hd), lambda b, g, c: (b, g, 0, 0)))
        args.append(s0)
    o, s = pl.pallas_call(
        functools.partial(_delta_kernel, hp=hp, L=L, has_state=s0 is not None),
        grid=(bsz, hg, nc),
        in_specs=in_specs,
        out_specs=[pl.BlockSpec((1, L, vw), lambda b, g, c: (b, c, g)),
                   pl.BlockSpec((1, nh, hd, hd), lambda b, g, c: (b, g, 0, 0))],
        out_shape=[jax.ShapeDtypeStruct((bsz, t, vdim), BF16),
                   jax.ShapeDtypeStruct((bsz, vh, hd, hd), F32)],
        scratch_shapes=[pltpu.VMEM((hb + L, qw), F32), pltpu.VMEM((hb + L, qw), F32),
                        pltpu.VMEM((hb + L, vw), F32)],
        compiler_params=_cparams(("parallel", "parallel", "arbitrary")),
        name="delta_rule",
    )(*args)
    return o, s


def _attn_kernel(q_ref, k0_ref, k1_ref, k2_ref, v0_ref, v1_ref, v2_ref, bias_ref, o_ref, *, nkv, grp, masked):
    hd = SWA_HEAD_DIM
    pw = 2 * hd
    npair = grp // 2
    nq = q_ref.shape[1]
    nk = k0_ref.shape[1] * 3
    nkp = bias_ref.shape[-1]
    c = pl.program_id(1)
    q = q_ref[0]
    zrows = jnp.zeros((nkp - nk, k0_ref.shape[2]), F32)
    k = jnp.concatenate([k0_ref[0], k1_ref[0], k2_ref[0], zrows], axis=0)
    v = jnp.concatenate([v0_ref[0], v1_ref[0], v2_ref[0], zrows], axis=0)
    low = lax.broadcasted_iota(jnp.int32, (nkp, pw), 1) < hd
    if masked:
        col = lax.broadcasted_iota(jnp.int32, (npair * nq, nkp), 1)
        valid = jnp.logical_or(col >= nk, c * CHUNK - WINDOW + col >= 0)
    ones = jnp.ones((nkp, pw), BF16)

    def halves(x, n):
        blk = x[:, (n // 2) * pw:(n // 2 + 1) * pw]
        swp = pltpu.roll(blk, hd, axis=1)
        lo_src, hi_src = (blk, swp) if n % 2 == 0 else (swp, blk)
        return (jnp.where(low, lo_src, 0.0).astype(BF16), jnp.where(low, 0.0, hi_src).astype(BF16))

    heads = range(nkv)
    kh = [halves(k, n) for n in heads]
    vh = [halves(v, n) for n in heads]
    q2 = [jnp.concatenate([q[:, (n * grp + 2 * a) * hd:(n * grp + 2 * a + 2) * hd] for a in range(npair)],
                          axis=0).astype(BF16) for n in heads]
    units = [(n, par) for n in heads for par in range(2)]
    ss = []
    for n, par in units:
        s = _mm_nt(q2[n], kh[n][par]) * (hd ** -0.5) + bias_ref[n, par]
        ss.append(jnp.where(valid, s, NEG_INF) if masked else s)
    ps = [jnp.exp(s - jnp.max(s, axis=-1, keepdims=True)).astype(BF16) for s in ss]
    dens = [jnp.dot(p, ones, preferred_element_type=F32) for p in ps]
    avs = [jnp.dot(ps[i], vh[n][par], preferred_element_type=F32) for i, (n, par) in enumerate(units)]
    for n in heads:
        o2 = avs[2 * n] / dens[2 * n] + avs[2 * n + 1] / dens[2 * n + 1]
        for a in range(npair):
            o_ref[0, :, (n * grp + 2 * a) * hd:(n * grp + 2 * a + 2) * hd] = o2[a * nq:(a + 1) * nq].astype(o_ref.dtype)


def _t5_bucket(rel):
    half = REL_BUCKETS // 2
    max_exact = half // 2
    a = jnp.abs(rel)
    af = jnp.maximum(a, 1).astype(F32)
    large = max_exact + (jnp.log(af / max_exact) / math.log(REL_MAX_DIST / max_exact)
                         * (half - max_exact)).astype(jnp.int32)
    large = jnp.minimum(large, half - 1)
    return jnp.where(rel > 0, half, 0) + jnp.where(a < max_exact, a, large)


def _bias_table(rel_bias, sinks, nkv, grp, n_q, n_k, n_kp):
    npair = grp // 2
    rel = jnp.arange(n_k)[None, :] - WINDOW - jnp.arange(n_q)[:, None]
    onehot = (_t5_bucket(rel)[..., None] == jnp.arange(rel_bias.shape[0])).astype(F32)
    bias = jnp.einsum("qkb,bh->qkh", onehot, rel_bias.astype(F32), precision=lax.Precision.HIGHEST)
    bias = jnp.transpose(bias, (2, 0, 1)).reshape(nkv, npair, 2, n_q, n_k)
    bias = jnp.transpose(bias, (0, 2, 1, 3, 4)).reshape(nkv, 2, npair * n_q, n_k)
    sink = jnp.transpose(sinks.astype(F32).reshape(nkv, npair, 2), (0, 2, 1))
    sink = jnp.repeat(sink, n_q, axis=2).reshape(nkv, 2, npair * n_q, 1)
    pad = jnp.full((nkv, 2, npair * n_q, n_kp - n_k - 1), NEG_INF, F32)
    return jnp.concatenate([bias, sink, pad], axis=-1)


def swa_attention(q_src, k_src, v_src, kv_col, rel_bias, sinks, nkv, masked):
    hd = SWA_HEAD_DIM
    nheads = sinks.shape[0]
    grp = nheads // nkv
    assert grp % 2 == 0 and nkv % 2 == 0, "heads are processed in lane-tile pairs"
    qd, kvd = nheads * hd, nkv * hd
    bsz = q_src.shape[0]
    t = k_src.shape[1] if masked else k_src.shape[1] - WINDOW
    nc = t // CHUNK
    nkp = 2 * LANES
    bias = _bias_table(rel_bias, sinks, nkv, grp, CHUNK, WINDOW + CHUNK, nkp)
    if masked:
        rows = [lambda b, c, j=j: jnp.maximum(c + j - 2, 0) for j in range(3)]
    else:
        rows = [lambda b, c, j=j: c + j for j in range(3)]
    kspecs = [pl.BlockSpec((1, CHUNK, kvd), lambda b, c, r=r: (b, r(b, c), kv_col[0])) for r in rows]
    vspecs = [pl.BlockSpec((1, CHUNK, kvd), lambda b, c, r=r: (b, r(b, c), kv_col[1])) for r in rows]
    return pl.pallas_call(
        functools.partial(_attn_kernel, nkv=nkv, grp=grp, masked=masked),
        grid=(bsz, nc),
        in_specs=[pl.BlockSpec((1, CHUNK, qd), lambda b, c: (b, c, 0))] + kspecs + vspecs
        + [pl.BlockSpec((nkv, 2, grp // 2 * CHUNK, nkp), lambda b, c: (0, 0, 0, 0))],
        out_specs=pl.BlockSpec((1, CHUNK, qd), lambda b, c: (b, c, 0)),
        out_shape=jax.ShapeDtypeStruct((bsz, t, qd), BF16),
        compiler_params=_cparams(("parallel", "arbitrary")),
        name="swa_attention",
    )(q_src, k_src, k_src, k_src, v_src, v_src, v_src, bias)


def _delta_heads_per_step(qh):
    return min(16, qh)


def _run_group(x, caches, w, wb, first_chunk):
    conv_a, delta_s, delta_conv, sconv, swa_k, swa_v = caches
    bsz, t, d = x.shape
    depth = w["norm_mix_pre"].shape[0]
    xf = x.reshape(bsz * t, d)
    new = {k: [] for k in ("conv_a", "ds", "dc", "sc", "k", "v")}
    for i in range(depth):
        mix, j = i % 4, i // 4
        g_pre, g_post, g_ffn = w["norm_mix_pre"][i], w["norm_mix_post"][i], w["norm_ffn_pre"][i]
        if mix == 0:
            width = w["conv_a_dw"].shape[1]
            cache = conv_a[j] if conv_a is not None else jnp.zeros((bsz, width - 1, d), F32)
            x3, tail, hf = conformer_mixer(xf.reshape(bsz, t, d), g_pre, wb["conv_a_w1"], w["conv_a_b1"][j],
                                           cache, w["conv_a_dw"][j], w["conv_a_dw_b"][j], w["conv_a_ln_g"][j],
                                           w["conv_a_ln_b"][j], wb["conv_a_w2"], j, w["conv_a_b2"][j], g_post,
                                           g_ffn)
            xf, hf = x3.reshape(bsz * t, d), hf.reshape(bsz * t, d)
            new["conv_a"].append(tail)
        elif mix == 1:
            w_in = w["delta_w_in"][j]
            vh = w["delta_a_log"].shape[1]
            vdim = vh * DN_HEAD_DIM
            qkvd = w["delta_conv_w"].shape[2]
            qh = (qkvd - vdim) // (2 * DN_HEAD_DIM)
            hp = _delta_heads_per_step(qh)
            hg = qh // hp
            (qkv,) = matmul_cols(hm, [(wb["delta_w_in"], j, 0)], qkvd, _comb_id, [F32])
            (z,) = matmul_cols(hm, [(wb["delta_w_in"], j, qkvd)], vdim, _comb_id, [F32])

            def lanes4(be, bo, ae, ao):
                pad = [(0, 0)] * (be.ndim - 1) + [(0, 32 - hp)]
                return jnp.concatenate([jnp.pad(a, pad) for a in (be, bo, ae, ao)], axis=-1)

            w_b = w_in[:, qkvd + vdim:qkvd + vdim + vh].reshape(d, hg, hp, 2)
            w_a = w_in[:, qkvd + vdim + vh:].reshape(d, hg, hp, 2)
            w_ba = lanes4(w_b[..., 0], w_b[..., 1], w_a[..., 0], w_a[..., 1]).reshape(d, hg * LANES)
            (ba,) = matmul_cols(hm, [(w_ba.astype(BF16)[None], 0, 0)], hg * LANES, _comb_id, [F32], tn=LANES)
            zl = jnp.zeros((hg, hp), F32)
            alog = w["delta_a_log"][j].reshape(hg, hp, 2)
            dtb = w["delta_dt_bias"][j].reshape(hg, hp, 2)
            par = jnp.stack([lanes4(zl, zl, alog[..., 0], alog[..., 1]),
                             lanes4(zl, zl, dtb[..., 0], dtb[..., 1])], axis=1)
            par = jnp.pad(par, ((0, 0), (0, SUBLANES - 2), (0, 0)))
            cw = w["delta_conv_w"][j]
            cache = delta_conv[j] if delta_conv is not None else jnp.zeros((bsz, cw.shape[0] - 1, qkvd), F32)
            qkv3 = qkv.reshape(bsz, t, qkvd)
            o, s_new = delta_rule(qkv3, z.reshape(bsz, t, vdim), ba.reshape(bsz, t, hg * LANES), cache, cw, par,
                                  w["delta_norm_g"][j], delta_s[j] if delta_s is not None else None, hp)
            xf, hf = matmul_norm_residual(o.reshape(bsz * t, vdim), wb["delta_w_out"], j, g_post, xf, g_ffn)
            new["ds"].append(s_new)
            new["dc"].append(qkv3[:, -(cw.shape[0] - 1):])
        elif mix == 2:
            w_in = wb["sconv_w_in"]
            bg, gx = matmul_cols(hm, [(w_in, j, 0), (w_in, j, d), (w_in, j, 2 * d)], d, _comb_sconv, [F32, F32],
                                 tm=1024, tn=512)
            gx = gx.reshape(bsz, t, d)
            cw = w["sconv_w"][j]
            cache = sconv[j] if sconv is not None else jnp.zeros((bsz, cw.shape[0] - 1, d), F32)
            x3, hf = sconv_tail(gx, cache, bg.reshape(bsz, t, d), cw, wb["sconv_w_out"], j, g_post, g_ffn,
                                xf.reshape(bsz, t, d))
            xf, hf = x3.reshape(bsz * t, d), hf.reshape(bsz * t, d)
            new["sc"].append(gx[:, -(cw.shape[0] - 1):])
        else:
            nheads = w["swa_sinks"].shape[1]
            qd = nheads * SWA_HEAD_DIM
            kvd = (w["swa_w_qkv"].shape[2] - qd) // 2
            nkv = kvd // SWA_HEAD_DIM
            (qkv,) = matmul_cols(hm, [(wb["swa_w_qkv"], j, 0)], qd + 2 * kvd, _comb_id, [F32])
            qkv = qkv.reshape(bsz, t, qd + 2 * kvd)
            if first_chunk:
                o = swa_attention(qkv, qkv, qkv, (qd // kvd, qd // kvd + 1), w["rel_bias"], w["swa_sinks"][j],
                                  nkv, True)
                k_ext, v_ext = qkv[:, :, qd:qd + kvd], qkv[:, :, qd + kvd:]
            else:
                k_ext = jnp.concatenate([swa_k[j].reshape(bsz, WINDOW, kvd), qkv[:, :, qd:qd + kvd]], axis=1)
                v_ext = jnp.concatenate([swa_v[j].reshape(bsz, WINDOW, kvd), qkv[:, :, qd + kvd:]], axis=1)
                o = swa_attention(qkv, k_ext, v_ext, (0, 0), w["rel_bias"], w["swa_sinks"][j], nkv, False)
            xf, hf = matmul_norm_residual(o.reshape(bsz * t, qd), wb["swa_w_out"], j, g_post, xf, g_ffn)
            new["k"].append(k_ext[:, -WINDOW:].reshape(bsz, WINDOW, nkv, SWA_HEAD_DIM))
            new["v"].append(v_ext[:, -WINDOW:].reshape(bsz, WINDOW, nkv, SWA_HEAD_DIM))
        g_next = w["norm_mix_pre"][i + 1] if i + 1 < depth else None
        xf, hm = ffn(hf, xf, wb["ffn_w_gate"], wb["ffn_w_up"], wb["ffn_w_down"], i, w["norm_ffn_post"][i], g_next)
    return xf.reshape(bsz, t, d), tuple(jnp.stack(new[k]) for k in ("conv_a", "ds", "dc", "sc", "k", "v"))


def kernel(x_prompt, x_sample, cache_conv_a, state_delta_s, state_delta_conv, cache_sconv, cache_swa_k, cache_swa_v, rel_bias, norm_mix_pre, norm_mix_post, norm_ffn_pre, norm_ffn_post, ffn_w_gate, ffn_w_up, ffn_w_down, conv_a_w1, conv_a_b1, conv_a_dw, conv_a_dw_b, conv_a_ln_g, conv_a_ln_b, conv_a_w2, conv_a_b2, delta_w_in, delta_conv_w, delta_a_log, delta_dt_bias, delta_norm_g, delta_w_out, sconv_w_in, sconv_w, sconv_w_out, swa_w_qkv, swa_sinks, swa_w_out):
    w = {
        "rel_bias": rel_bias, "norm_mix_pre": norm_mix_pre, "norm_mix_post": norm_mix_post,
        "norm_ffn_pre": norm_ffn_pre, "norm_ffn_post": norm_ffn_post, "ffn_w_gate": ffn_w_gate,
        "ffn_w_up": ffn_w_up, "ffn_w_down": ffn_w_down, "conv_a_w1": conv_a_w1, "conv_a_b1": conv_a_b1,
        "conv_a_dw": conv_a_dw, "conv_a_dw_b": conv_a_dw_b, "conv_a_ln_g": conv_a_ln_g,
        "conv_a_ln_b": conv_a_ln_b, "conv_a_w2": conv_a_w2, "conv_a_b2": conv_a_b2,
        "delta_w_in": delta_w_in, "delta_conv_w": delta_conv_w, "delta_a_log": delta_a_log,
        "delta_dt_bias": delta_dt_bias, "delta_norm_g": delta_norm_g, "delta_w_out": delta_w_out,
        "sconv_w_in": sconv_w_in, "sconv_w": sconv_w, "sconv_w_out": sconv_w_out,
        "swa_w_qkv": swa_w_qkv, "swa_sinks": swa_sinks, "swa_w_out": swa_w_out,
    }
    wb = {name: w[name].astype(BF16) for name in (
        "ffn_w_gate", "ffn_w_up", "ffn_w_down", "conv_a_w1", "conv_a_w2", "delta_w_out",
        "sconv_w_in", "sconv_w_out", "swa_w_qkv", "swa_w_out")}
    n_main = delta_conv_w.shape[2] + delta_a_log.shape[1] * DN_HEAD_DIM
    wb["delta_w_in"] = delta_w_in[:, :, :n_main].astype(BF16)
    y_p, (ca_p, ds_p, dc_p, sc_p, k_p, v_p) = _run_group(x_prompt, (None,) * 6, w, wb, True)
    y_s, (ca_s, ds_s, dc_s, sc_s, k_s, v_s) = _run_group(
        x_sample, (cache_conv_a, state_delta_s, state_delta_conv, cache_sconv, cache_swa_k, cache_swa_v), w, wb,
        False)
    return (y_p, y_s, ca_p, ca_s, ds_p, ds_s, dc_p, dc_s, sc_p, sc_s, k_p, k_s, v_p, v_s)
```
